```python
import jax, jax.numpy as jnp
from jax import lax
import numpy as np

D_MODEL = 4096
BATCH = 4
SEQ = 2048
DEPTH = 2
DEC_BATCH = 8
DEC_SEQ = 4
PAST_LEN = 16384
PAGE_SIZE = 128

N_MIXERS = 2
N_A_LAYERS = (DEPTH + 1) // 2
N_B_LAYERS = DEPTH // 2
EPS = 1e-6
CHUNK = 128
D_GATE = D_MODEL
GATE_GROUPS = 32
GROUP_DIM = D_GATE // GATE_GROUPS
N_HEADS = 32
HEAD_DIM = D_MODEL // N_HEADS
Q_BLOCK = 128
SB_SCALE = HEAD_DIM ** -0.5
SB_BIAS_INIT = -6.0
PEER_HEADS = 8
PEER_NKEYS = 128
PEER_EXPERTS = PEER_NKEYS * PEER_NKEYS
PEER_TOPK = 16
PEER_DK = 256
PEER_E_PER_TOKEN = PEER_HEADS * PEER_TOPK
PEER_TOKEN_BLOCK = 64

kernel_name = 'hybrid_chunkgmlp_stickbreak_peer_step'


def rmsnorm(x, g):
    xf = x.astype(jnp.float32)
    y = xf * lax.rsqrt(jnp.mean(xf * xf, axis=-1, keepdims=True) + EPS)
    return (y * g.astype(jnp.float32)).astype(x.dtype)


def chunk_gmlp(h, w_in, g_norm, w_s, b_s, w_out):
    B, T, _ = h.shape
    n_chunks = -(-T // CHUNK)
    t_pad = n_chunks * CHUNK
    if t_pad != T:
        h = jnp.pad(h, ((0, 0), (0, t_pad - T), (0, 0)))
    z = jax.nn.gelu(h @ w_in, approximate=False)
    u, v = jnp.split(z, 2, axis=-1)
    vn = rmsnorm(v, g_norm)
    causal = jnp.tril(jnp.ones((CHUNK, CHUNK), dtype=bool))
    w_m = jnp.where(causal[None], w_s, 0).astype(h.dtype)
    vc = vn.reshape(B, n_chunks, CHUNK, GATE_GROUPS, GROUP_DIM)
    s = jnp.einsum('gts,bcsgd->bctgd', w_m, vc) + b_s.T[:, :, None].astype(h.dtype)
    y = (u * s.reshape(B, t_pad, D_GATE)) @ w_out
    last = ((T - 1) // CHUNK) * CHUNK
    return y[:, :T], vn[:, last:T]


def stick_breaking_weights(z, mask):
    log1m = jnp.where(mask, -jax.nn.softplus(z), 0.0)
    rev = lax.cumsum(log1m, axis=z.ndim - 1, reverse=True)
    log_a = jax.nn.log_sigmoid(z) + (rev - log1m)
    return jnp.where(mask, jnp.exp(log_a), 0.0)


def stick_breaking_prompt(h, w_qkv, w_o, bias):
    B, T, _ = h.shape
    qkv = (h @ w_qkv).reshape(B, T, 3, N_HEADS, HEAD_DIM)
    q, k, v = qkv[:, :, 0], qkv[:, :, 1], qkv[:, :, 2]
    nb = T // Q_BLOCK
    q_blocks = q.reshape(B, nb, Q_BLOCK, N_HEADS, HEAD_DIM).swapaxes(0, 1)
    pos_blocks = jnp.arange(T).reshape(nb, Q_BLOCK)
    k_pos = jnp.arange(T)
    b_h = bias.astype(jnp.float32)[None, :, None, None]

    def block(args):
        q_b, p_b = args
        z = jnp.einsum('bqhd,bkhd->bhqk', q_b, k).astype(jnp.float32) * SB_SCALE + b_h
        mask = k_pos[None, :] < p_b[:, None]
        a = stick_breaking_weights(z, mask).astype(v.dtype)
        return jnp.einsum('bhqk,bkhd->bqhd', a, v)

    o = lax.map(block, (q_blocks, pos_blocks)).swapaxes(0, 1).reshape(B, T, N_HEADS * HEAD_DIM)
    return o @ w_o, k, v


def stick_breaking_sample(h, w_qkv, w_o, bias, cache_k, cache_v, page_table, li):
    Bd, T, _ = h.shape
    qkv = (h @ w_qkv).reshape(Bd, T, 3, N_HEADS, HEAD_DIM)
    q, k, v = qkv[:, :, 0], qkv[:, :, 1], qkv[:, :, 2]
    past = page_table.shape[1] * PAGE_SIZE
    q_pos = past + jnp.arange(T)
    k_pos = jnp.arange(past + T)
    mask = k_pos[None, :] < q_pos[:, None]
    b_h = bias.astype(jnp.float32)[:, None, None]

    def one(args):
        q_i, k_i, v_i, pt_i = args
        k_past = cache_k[li, pt_i].reshape(past, N_HEADS, HEAD_DIM)
        v_past = cache_v[li, pt_i].reshape(past, N_HEADS, HEAD_DIM)
        z = jnp.concatenate([jnp.einsum('qhd,khd->hqk', q_i, k_past),
                             jnp.einsum('qhd,khd->hqk', q_i, k_i)], axis=-1)
        z = z.astype(jnp.float32) * SB_SCALE + b_h
        a = stick_breaking_weights(z, mask).astype(v_i.dtype)
        return (jnp.einsum('hqk,khd->qhd', a[..., :past], v_past)
                + jnp.einsum('hqk,khd->qhd', a[..., past:], v_i))

    o = lax.map(one, (q, k, v, page_table)).reshape(Bd, T, N_HEADS * HEAD_DIM)
    return o @ w_o, k, v


def peer(h, w_q, subkeys, u_tab, v_tab):
    B, T, D = h.shape
    n = B * T
    xf = h.reshape(n, D)
    q = (xf @ w_q).reshape(n, PEER_HEADS, 2, PEER_DK // 2)
    s = jnp.einsum('nhpd,hpkd->nhpk', q, subkeys).astype(jnp.float32)
    s1, i1 = lax.top_k(s[:, :, 0], PEER_TOPK)
    s2, i2 = lax.top_k(s[:, :, 1], PEER_TOPK)
    cand = (s1[..., :, None] + s2[..., None, :]).reshape(n, PEER_HEADS, PEER_TOPK * PEER_TOPK)
    cidx = (i1[..., :, None] * PEER_NKEYS + i2[..., None, :]).reshape(n, PEER_HEADS, PEER_TOPK * PEER_TOPK)
    top, sel = lax.top_k(cand, PEER_TOPK)
    eidx = jnp.take_along_axis(cidx, sel, axis=-1).reshape(n, PEER_E_PER_TOKEN)
    gate = jax.nn.softmax(top, axis=-1).reshape(n, PEER_E_PER_TOKEN)
    tb = min(PEER_TOKEN_BLOCK, n)
    nb = -(-n // tb)
    pad = nb * tb - n
    if pad:
        xf = jnp.pad(xf, ((0, pad), (0, 0)))
        eidx = jnp.pad(eidx, ((0, pad), (0, 0)))
        gate = jnp.pad(gate, ((0, pad), (0, 0)))

    def block(args):
        xb, ib, gb = args
        act = jax.nn.gelu(jnp.einsum('ned,nd->ne', u_tab[ib], xb).astype(jnp.float32), approximate=False)
        return jnp.einsum('ne,ned->nd', (gb * act).astype(xb.dtype), v_tab[ib])

    out = lax.map(block, (xf.reshape(nb, tb, D), eidx.reshape(nb, tb, PEER_E_PER_TOKEN),
                          gate.reshape(nb, tb, PEER_E_PER_TOKEN)))
    return out.reshape(nb * tb, D)[:n].reshape(B, T, D)


def setup_inputs(seed: int = 0) -> dict:
    key = jax.random.key(seed)
    ks = jax.random.split(key, 21)
    f32 = jnp.float32

    def nrm(k, shape, scale):
        return jax.random.normal(k, shape, f32) * scale

    n_pages = PAST_LEN // PAGE_SIZE
    n_used = DEC_BATCH * n_pages
    n_pool = n_used + max(1, n_used // 4)
    return {
        'x_prompt': nrm(ks[0], (BATCH, SEQ, D_MODEL), 1.0),
        'x_sample': nrm(ks[1], (DEC_BATCH, DEC_SEQ, D_MODEL), 1.0),
        'cache_k': nrm(ks[2], (N_B_LAYERS, n_pool, PAGE_SIZE, N_HEADS, HEAD_DIM), 1.0),
        'cache_v': nrm(ks[3], (N_B_LAYERS, n_pool, PAGE_SIZE, N_HEADS, HEAD_DIM), 1.0),
        'page_table': jax.random.permutation(ks[4], n_pool)[:n_used].reshape(DEC_BATCH, n_pages).astype(jnp.int32),
        'norm_mix': 1.0 + nrm(ks[5], (DEPTH, D_MODEL), 0.02),
        'norm_ffn': 1.0 + nrm(ks[6], (DEPTH, D_MODEL), 0.02),
        'norm_final': 1.0 + nrm(ks[7], (D_MODEL,), 0.02),
        'cm_w_in': nrm(ks[8], (N_A_LAYERS, D_MODEL, 2 * D_GATE), D_MODEL ** -0.5),
        'cm_norm': 1.0 + nrm(ks[9], (N_A_LAYERS, D_GATE), 0.02),
        'cm_w_s': nrm(ks[10], (N_A_LAYERS, GATE_GROUPS, CHUNK, CHUNK), 0.5 * CHUNK ** -0.5),
        'cm_b_s': 1.0 + nrm(ks[11], (N_A_LAYERS, GATE_GROUPS, CHUNK), 0.1),
        'cm_w_out': nrm(ks[12], (N_A_LAYERS, D_GATE, D_MODEL), D_GATE ** -0.5),
        'sb_w_qkv': nrm(ks[13], (N_B_LAYERS, D_MODEL, 3 * N_HEADS * HEAD_DIM), D_MODEL ** -0.5),
        'sb_bias': SB_BIAS_INIT + nrm(ks[19], (N_B_LAYERS, N_HEADS), 0.1),
        'sb_w_o': nrm(ks[14], (N_B_LAYERS, N_HEADS * HEAD_DIM, D_MODEL), (N_HEADS * HEAD_DIM) ** -0.5),
        'peer_w_q': nrm(ks[15], (DEPTH, D_MODEL, PEER_HEADS * PEER_DK), D_MODEL ** -0.5),
        'peer_subkeys': nrm(ks[16], (DEPTH, PEER_HEADS, 2, PEER_NKEYS, PEER_DK // 2), (PEER_DK // 2) ** -0.5),
        'peer_u': nrm(ks[17], (DEPTH, PEER_EXPERTS, D_MODEL), D_MODEL ** -0.5),
        'peer_v': nrm(ks[18], (DEPTH, PEER_EXPERTS, D_MODEL), PEER_E_PER_TOKEN ** -0.5),
    }


def reference(x_prompt, x_sample, cache_k, cache_v, page_table, norm_mix, norm_ffn, norm_final,
              cm_w_in, cm_norm, cm_w_s, cm_b_s, cm_w_out, sb_w_qkv, sb_bias, sb_w_o,
              peer_w_q, peer_subkeys, peer_u, peer_v):
    xp, xs = x_prompt, x_sample
    gate_p, gate_s, kp_rows, vp_rows, ks_rows, vs_rows = [], [], [], [], [], []
    for layer in range(DEPTH):
        hp = rmsnorm(xp, norm_mix[layer])
        hs = rmsnorm(xs, norm_mix[layer])
        li = layer // N_MIXERS
        if layer % N_MIXERS == 0:
            dp, rp = chunk_gmlp(hp, cm_w_in[li], cm_norm[li], cm_w_s[li], cm_b_s[li], cm_w_out[li])
            ds, rs = chunk_gmlp(hs, cm_w_in[li], cm_norm[li], cm_w_s[li], cm_b_s[li], cm_w_out[li])
            gate_p.append(rp)
            gate_s.append(rs)
        else:
            dp, kp, vp = stick_breaking_prompt(hp, sb_w_qkv[li], sb_w_o[li], sb_bias[li])
            ds, kq, vq = stick_breaking_sample(hs, sb_w_qkv[li], sb_w_o[li], sb_bias[li], cache_k, cache_v, page_table, li)
            kp_rows.append(kp)
            vp_rows.append(vp)
            ks_rows.append(kq)
            vs_rows.append(vq)
        xp = xp + dp
        xs = xs + ds
        u_tab, v_tab = peer_u[layer], peer_v[layer]
        xp = xp + peer(rmsnorm(xp, norm_ffn[layer]), peer_w_q[layer], peer_subkeys[layer], u_tab, v_tab)
        xs = xs + peer(rmsnorm(xs, norm_ffn[layer]), peer_w_q[layer], peer_subkeys[layer], u_tab, v_tab)
    y_prompt = rmsnorm(xp, norm_final)
    y_sample = rmsnorm(xs, norm_final)
    gate_rows_prompt = jnp.stack(gate_p)
    gate_rows_sample = jnp.stack(gate_s)
    k_rows_prompt = jnp.stack(kp_rows)
    v_rows_prompt = jnp.stack(vp_rows)
    k_rows_sample = jnp.stack(ks_rows)
    v_rows_sample = jnp.stack(vs_rows)
    return (y_prompt, y_sample, gate_rows_prompt, gate_rows_sample, k_rows_prompt, v_rows_prompt, k_rows_sample, v_rows_sample)
```

```python
import functools

import jax
import jax.numpy as jnp
import numpy as np
from jax import lax
from jax.experimental import pallas as pl
from jax.experimental.pallas import tpu as pltpu

EPS = 1e-6
CHUNK = 128
PEER_TOPK = 16
V7X_VMEM_LIMIT_BYTES = 56 * 1024 * 1024
LANES = 128
F32 = jnp.float32
BF16 = jnp.bfloat16
SQRT_HALF = float(np.sqrt(0.5))


def _params(*sem):
    return pltpu.CompilerParams(dimension_semantics=sem,
                                vmem_limit_bytes=V7X_VMEM_LIMIT_BYTES)


def _gelu(x):
    return 0.5 * x * (1.0 + lax.erf(x * SQRT_HALF))


def _softplus(z):
    return jnp.maximum(z, 0.0) + jnp.log1p(jnp.exp(-jnp.abs(z)))


def _split_bf16(x):
    hi = x.astype(BF16)
    lo = (x - hi.astype(F32)).astype(BF16)
    return hi, lo


def _tile(n, pref):
    return pref if n % pref == 0 else n


def _rmsnorm_kernel(*refs, has_add, emit_sum):
    if has_add:
        x_ref, d_ref, g_ref, *outs = refs
        x = x_ref[...] + d_ref[...]
    else:
        x_ref, g_ref, *outs = refs
        x = x_ref[...]
    if emit_sum:
        outs[0][...] = x
    o_ref = outs[-1]
    ms = jnp.mean(x * x, axis=-1, keepdims=True)
    o_ref[...] = (x * lax.rsqrt(ms + EPS) * g_ref[...]).astype(o_ref.dtype)


def rmsnorm(x, g, *, add=None, emit_sum=False, out_dtype=BF16):
    m, d = x.shape
    tr = _tile(m, 256)
    row = pl.BlockSpec((tr, d), lambda i: (i, 0))
    ins = [x] + ([add] if add is not None else []) + [g.reshape(1, d)]
    in_specs = [row] * (len(ins) - 1) + [pl.BlockSpec((1, d), lambda i: (0, 0))]
    out_shape = [jax.ShapeDtypeStruct((m, d), out_dtype)]
    if emit_sum:
        out_shape = [jax.ShapeDtypeStruct((m, d), F32)] + out_shape
    res = pl.pallas_call(
        functools.partial(_rmsnorm_kernel, has_add=add is not None, emit_sum=emit_sum),
        grid=(m // tr,), in_specs=in_specs, out_specs=[row] * len(out_shape),
        out_shape=out_shape, compiler_params=_params("parallel"), name="rmsnorm")(*ins)
    return res if emit_sum else res[0]


def _matmul_kernel(*refs, has_res, gelu):
    a_ref, w_ref = refs[:2]
    outs = refs[3:] if has_res else refs[2:]
    acc = jnp.dot(a_ref[...], w_ref[...], preferred_element_type=F32)
    if has_res:
        acc = refs[2][...] + acc
    if gelu:
        acc = _gelu(acc)
    for o in outs:
        o[...] = acc.astype(o.dtype)


def matmul(a, w, *, n=None, col0=0, res=None, gelu=False, out_dtypes=(F32,)):
    m, k = a.shape
    n = w.shape[1] if n is None else n
    tm = _tile(m, 1024)
    tn = _tile(n, 512 if (res is not None or len(out_dtypes) > 1) else 1024)
    assert col0 % tn == 0
    cb = col0 // tn
    ins = [a, w] + ([res] if res is not None else [])
    tile = pl.BlockSpec((tm, tn), lambda i, j: (i, j))
    in_specs = [pl.BlockSpec((tm, k), lambda i, j: (i, 0)),
                pl.BlockSpec((k, tn), lambda i, j: (0, j + cb))]
    if res is not None:
        in_specs.append(tile)
    outs = pl.pallas_call(
        functools.partial(_matmul_kernel, has_res=res is not None, gelu=gelu),
        grid=(m // tm, n // tn), in_specs=in_specs,
        out_specs=[tile] * len(out_dtypes),
        out_shape=[jax.ShapeDtypeStruct((m, n), dt) for dt in out_dtypes],
        compiler_params=_params("parallel", "parallel"), name="matmul")(*ins)
    return outs if len(out_dtypes) > 1 else outs[0]


def _gating_kernel(u_ref, v_ref, gn_ref, ws_ref, b_ref, p_ref, vn_ref, *, groups, gd):
    v = v_ref[...]
    ms = jnp.mean(v * v, axis=-1, keepdims=True)
    vn = v * lax.rsqrt(ms + EPS) * gn_ref[...]
    vn_ref[...] = vn
    t_i = lax.broadcasted_iota(jnp.int32, (CHUNK, CHUNK), 0)
    s_i = lax.broadcasted_iota(jnp.int32, (CHUNK, CHUNK), 1)
    causal = s_i <= t_i
    for g in range(groups):
        cols = slice(g * gd, (g + 1) * gd)
        wm = jnp.where(causal, ws_ref[g], 0.0).astype(BF16)
        s = jnp.dot(wm, vn[:, cols].astype(BF16), preferred_element_type=F32)
        s = s + b_ref[:, cols]
        p_ref[:, cols] = (u_ref[:, cols] * s).astype(p_ref.dtype)


def gating(z, g_norm, w_s, b_s):
    rows, dg2 = z.shape
    dg = dg2 // 2
    groups = w_s.shape[0]
    gd = dg // groups
    b_full = jnp.repeat(b_s.T, gd, axis=1)
    half = lambda c: pl.BlockSpec((CHUNK, dg), lambda i: (i, c))
    return pl.pallas_call(
        functools.partial(_gating_kernel, groups=groups, gd=gd),
        grid=(rows // CHUNK,),
        in_specs=[half(0), half(1),
                  pl.BlockSpec((1, dg), lambda i: (0, 0)),
                  pl.BlockSpec((groups, CHUNK, CHUNK), lambda i: (0, 0, 0)),
                  pl.BlockSpec((CHUNK, dg), lambda i: (0, 0))],
        out_specs=[half(0), half(0)],
        out_shape=[jax.ShapeDtypeStruct((rows, dg), BF16),
                   jax.ShapeDtypeStruct((rows, dg), F32)],
        compiler_params=_params("parallel"), name="gating",
    )(z, z, g_norm.reshape(1, dg), w_s, b_full)


def _sb_prompt_kernel(bias_ref, q_ref, k_ref, v_ref, o_ref, *, tq, tk, scale):
    h = pl.program_id(1)
    qi = pl.program_id(2)
    q = q_ref[...]
    bias = bias_ref[h]
    q0 = qi * tq
    nkb = qi + 1 if tq == tk else (q0 + tq + tk - 1) // tk
    j_i = lax.broadcasted_iota(jnp.int32, (tk, tk), 0)
    s_i = lax.broadcasted_iota(jnp.int32, (tk, tk), 1)
    tri = jnp.where(j_i > s_i, 1.0, 0.0).astype(BF16)
    qpos = q0 + lax.broadcasted_iota(jnp.int32, (tq, tk), 0)
    koff = lax.broadcasted_iota(jnp.int32, (tq, tk), 1)

    def body(i, carry):
        c, acc = carry
        ks = pl.multiple_of((nkb - 1 - i) * tk, tk)
        kblk = k_ref[pl.ds(ks, tk), :]
        vblk = v_ref[pl.ds(ks, tk), :]
        z = lax.dot_general(q, kblk, (((1,), (1,)), ((), ())),
                            preferred_element_type=F32) * scale + bias
        mask = (ks + koff) < qpos
        sp = _softplus(z)
        l1m = jnp.where(mask, -sp, 0.0)
        hi, lo = _split_bf16(l1m)
        excl = (jnp.dot(hi, tri, preferred_element_type=F32)
                + jnp.dot(lo, tri, preferred_element_type=F32))
        a = jnp.where(mask, jnp.exp((z - sp) + excl + c), 0.0)
        acc = acc + jnp.dot(a.astype(BF16), vblk, preferred_element_type=F32)
        c = c + jnp.sum(l1m, axis=1, keepdims=True)
        return c, acc

    c0 = jnp.zeros((tq, 1), F32)
    acc0 = jnp.zeros((tq, q.shape[1]), F32)
    _, acc = lax.fori_loop(0, nkb, body, (c0, acc0))
    o_ref[...] = acc.astype(o_ref.dtype)


def sb_attention_prompt(q, k, v, bias, *, batch, heads):
    m, hd = q.shape
    t = m // batch
    dh = hd // heads
    tq = tk = _tile(t, 256)
    nq = t // tq
    qspec = pl.BlockSpec((tq, dh), lambda b, h, i: (b * nq + i, h))
    kvspec = pl.BlockSpec((t, dh), lambda b, h, i: (b, h))
    return pl.pallas_call(
        functools.partial(_sb_prompt_kernel, tq=tq, tk=tk, scale=dh ** -0.5),
        grid=(batch, heads, nq),
        in_specs=[pl.BlockSpec(memory_space=pltpu.SMEM), qspec, kvspec, kvspec],
        out_specs=qspec,
        out_shape=jax.ShapeDtypeStruct((m, hd), BF16),
        compiler_params=_params("parallel", "parallel", "parallel"), name="sb_prompt",
    )(bias.astype(F32), q, k, v)


def _sb_sample_kernel(pt_ref, bias_ref, qbd_ref, knew_ref, vnew_ref, kpage_ref, vpage_ref,
                      o_ref, c_ref, acc_ref, *, heads, dh, tq, page, scale):
    del pt_ref
    j = pl.program_id(1)
    nc = qbd_ref.shape[1]
    r_i = lax.broadcasted_iota(jnp.int32, (page, page), 0)
    j_i = lax.broadcasted_iota(jnp.int32, (page, page), 1)
    tri = jnp.where(j_i > r_i, 1.0, 0.0).astype(BF16)

    def block(kblk, vblk, mask):
        z = jnp.dot(kblk, qbd_ref[...], preferred_element_type=F32) * scale + bias_ref[...]
        sp = _softplus(z)
        l1m = -sp if mask is None else jnp.where(mask, -sp, 0.0)
        hi, lo = _split_bf16(l1m)
        excl = (jnp.dot(tri, hi, preferred_element_type=F32)
                + jnp.dot(tri, lo, preferred_element_type=F32))
        a = jnp.exp((z - sp) + excl + c_ref[...])
        if mask is not None:
            a = jnp.where(mask, a, 0.0)
        acc_ref[...] += jnp.dot(a.T.astype(BF16), vblk, preferred_element_type=F32)
        c_ref[...] += jnp.sum(l1m, axis=0, keepdims=True)

    @pl.when(j == 0)
    def _():
        c_ref[...] = jnp.zeros_like(c_ref)
        acc_ref[...] = jnp.zeros_like(acc_ref)
        key = lax.broadcasted_iota(jnp.int32, (page, nc), 0)
        qry = lax.broadcasted_iota(jnp.int32, (page, nc), 1) % tq
        block(knew_ref[...], vnew_ref[...], key < qry)

    @pl.when(j > 0)
    def _():
        block(kpage_ref[...].astype(BF16), vpage_ref[...].astype(BF16), None)

    @pl.when(j == pl.num_programs(1) - 1)
    def _():
        for h in range(heads):
            cols = slice(h * dh, (h + 1) * dh)
            o_ref[:, cols] = acc_ref[h * tq:(h + 1) * tq, cols].astype(o_ref.dtype)


def sb_attention_sample(q, k_new, v_new, bias, cache_k, cache_v, page_table, li, *, heads):
    bd, tq, hd = q.shape
    dh = hd // heads
    _, pool, page, _, _ = cache_k.shape
    n_pages = page_table.shape[1]
    nc = -(-heads * tq // LANES) * LANES
    q4 = q.reshape(bd, tq, heads, dh)
    eye = jnp.eye(heads, dtype=F32)
    qbd = jnp.einsum("bthd,hg->bhdgt", q4, eye).reshape(bd, hd, heads * tq)
    qbd = jnp.pad(qbd, ((0, 0), (0, 0), (0, nc - heads * tq))).astype(BF16)
    bias_l = jnp.pad(jnp.repeat(bias.astype(F32), tq), (0, nc - heads * tq)).reshape(1, nc)
    pad_new = lambda x: jnp.pad(x, ((0, 0), (0, page - tq), (0, 0))).astype(BF16)
    ck = cache_k.reshape(cache_k.shape[0], pool, page, hd)
    cv = cache_v.reshape(cache_v.shape[0], pool, page, hd)

    def page_map(b, j, pt):
        return (li, pt[b, n_pages - jnp.maximum(j, 1)], 0, 0)

    per_b = lambda shape: pl.BlockSpec((None,) + shape, lambda b, j, pt: (b, 0, 0))
    grid_spec = pltpu.PrefetchScalarGridSpec(
        num_scalar_prefetch=1, grid=(bd, n_pages + 1),
        in_specs=[pl.BlockSpec((1, nc), lambda b, j, pt: (0, 0)),
                  per_b((hd, nc)), per_b((page, hd)), per_b((page, hd)),
                  pl.BlockSpec((None, None, page, hd), page_map),
                  pl.BlockSpec((None, None, page, hd), page_map)],
        out_specs=per_b((tq, hd)),
        scratch_shapes=[pltpu.VMEM((1, nc), F32), pltpu.VMEM((nc, hd), F32)])
    return pl.pallas_call(
        functools.partial(_sb_sample_kernel, heads=heads, dh=dh, tq=tq, page=page,
                          scale=dh ** -0.5),
        grid_spec=grid_spec, out_shape=jax.ShapeDtypeStruct((bd, tq, hd), F32),
        compiler_params=_params("parallel", "arbitrary"), name="sb_sample",
    )(page_table, bias_l, qbd, pad_new(k_new), pad_new(v_new), ck, cv)


def _topk_rows(s, k):
    r = s.shape[0]
    rows = lax.broadcasted_iota(jnp.int32, s.shape, 0)
    vals, ids = [], []
    for _ in range(k):
        m = jnp.max(s, axis=0, keepdims=True)
        idx = jnp.min(jnp.where(s == m, rows, r), axis=0, keepdims=True)
        vals.append(m)
        ids.append(idx)
        s = jnp.where(rows == idx, -jnp.inf, s)
    return jnp.concatenate(vals, axis=0), jnp.concatenate(ids, axis=0)


def _peer_route_kernel(q_ref, sub_ref, e_ref, g_ref, *, nk, topk):
    dkh = sub_ref.shape[-1]
    q = q_ref[...]
    tops = []
    for p in range(2):
        qh, ql = _split_bf16(q[:, p * dkh:(p + 1) * dkh])
        sh, sl = _split_bf16(sub_ref[p])
        nt = lambda a, b: lax.dot_general(a, b, (((1,), (1,)), ((), ())),
                                          preferred_element_type=F32)
        s = nt(sh, qh) + nt(sh, ql) + nt(sl, qh)
        tops.append(_topk_rows(s, topk))
    (s1, i1), (s2, i2) = tops
    cand = jnp.concatenate([s1[i:i + 1] + s2 for i in range(topk)], axis=0)
    cidx = jnp.concatenate([i1[i:i + 1] * nk + i2 for i in range(topk)], axis=0)
    rows = lax.broadcasted_iota(jnp.int32, cand.shape, 0)
    top, sel = _topk_rows(cand, topk)
    e_ref[...] = jnp.concatenate(
        [jnp.max(jnp.where(rows == sel[i:i + 1], cidx, -1), axis=0, keepdims=True)
         for i in range(topk)], axis=0)
    ex = jnp.exp(top - jnp.max(top, axis=0, keepdims=True))
    g_ref[...] = ex / jnp.sum(ex, axis=0, keepdims=True)


def peer_route(q, subkeys):
    m, _ = q.shape
    heads, _, nk, dkh = subkeys.shape
    tt = _tile(m, 256)
    out = pl.BlockSpec((PEER_TOPK, tt), lambda i, h: (h, i))
    return pl.pallas_call(
        functools.partial(_peer_route_kernel, nk=nk, topk=PEER_TOPK),
        grid=(m // tt, heads),
        in_specs=[pl.BlockSpec((tt, 2 * dkh), lambda i, h: (i, h)),
                  pl.BlockSpec((None, 2, nk, dkh), lambda i, h: (h, 0, 0, 0))],
        out_specs=[out, out],
        out_shape=[jax.ShapeDtypeStruct((heads * PEER_TOPK, m), jnp.int32),
                   jax.ShapeDtypeStruct((heads * PEER_TOPK, m), F32)],
        compiler_params=_params("parallel", "parallel"), name="peer_route")(q, subkeys)


def _peer_w_kernel(e_ref, g_ref, w_ref, *, nk):
    tb, hk = e_ref.shape
    key = lax.broadcasted_iota(jnp.int32, (nk, hk), 0)

    def body(t, _):
        e = e_ref[pl.ds(t, 1), :]
        g = g_ref[pl.ds(t, 1), :]
        a = jnp.where(key == e // nk, g, 0.0).astype(BF16)
        b = jnp.where(key == e % nk, 1.0, 0.0).astype(BF16)
        w = lax.dot_general(a, b, (((1,), (1,)), ((), ())), preferred_element_type=F32)
        w_ref[t] = w.astype(w_ref.dtype)
        return 0

    lax.fori_loop(0, tb, body, 0)


def peer_gate_matrix(eidx, gate, nk):
    m, hk = eidx.shape
    tb = _tile(m, 128)
    row = pl.BlockSpec((tb, hk), lambda i: (i, 0))
    w = pl.pallas_call(
        functools.partial(_peer_w_kernel, nk=nk),
        grid=(m // tb,), in_specs=[row, row],
        out_specs=pl.BlockSpec((tb, nk, nk), lambda i: (i, 0, 0)),
        out_shape=jax.ShapeDtypeStruct((m, nk, nk), BF16),
        compiler_params=_params("parallel"), name="peer_w")(eidx, gate)
    return w.reshape(m, nk * nk)


def _peer_dense_kernel(h_ref, u_ref, v_ref, w_ref, o_ref):
    j = pl.program_id(1)
    s = lax.dot_general(h_ref[...], u_ref[...], (((1,), (1,)), ((), ())),
                        preferred_element_type=F32)
    p = (w_ref[...].astype(F32) * _gelu(s)).astype(BF16)
    d = jnp.dot(p, v_ref[...], preferred_element_type=F32)

    @pl.when(j == 0)
    def _():
        o_ref[...] = d

    @pl.when(j > 0)
    def _():
        o_ref[...] += d


def peer_dense(h, u, v, w):
    m, d = h.shape
    e = u.shape[0]
    tm = _tile(m, 512)
    te = _tile(e, 512)
    return pl.pallas_call(
        _peer_dense_kernel, grid=(m // tm, e // te),
        in_specs=[pl.BlockSpec((tm, d), lambda i, j: (i, 0)),
                  pl.BlockSpec((te, d), lambda i, j: (j, 0)),
                  pl.BlockSpec((te, d), lambda i, j: (j, 0)),
                  pl.BlockSpec((tm, te), lambda i, j: (i, j))],
        out_specs=pl.BlockSpec((tm, d), lambda i, j: (i, 0)),
        out_shape=jax.ShapeDtypeStruct((m, d), F32),
        compiler_params=_params("parallel", "arbitrary"), name="peer_dense")(h, u, v, w)


def peer(h, w_q, subkeys, u, v):
    m = h.shape[0]
    nk = subkeys.shape[2]
    q = matmul(h, w_q)
    mp = -(-m // LANES) * LANES
    e_t, g_t = peer_route(jnp.pad(q, ((0, mp - m), (0, 0))), subkeys)
    w = peer_gate_matrix(e_t.T[:m], g_t.T[:m], nk)
    return peer_dense(h, u, v, w)


def kernel(x_prompt, x_sample, cache_k, cache_v, page_table, norm_mix, norm_ffn, norm_final,
           cm_w_in, cm_norm, cm_w_s, cm_b_s, cm_w_out, sb_w_qkv, sb_bias, sb_w_o,
           peer_w_q, peer_subkeys, peer_u, peer_v):
    b, t, d = x_prompt.shape
    bd, ts, _ = x_sample.shape
    depth = norm_mix.shape[0]
    heads = cache_k.shape[3]
    hd = heads * cache_k.shape[4]
    xp = x_prompt.reshape(b * t, d)
    xs = x_sample.reshape(bd * ts, d)
    assert t % CHUNK == 0 and ts <= CHUNK
    to_bf = lambda a: a.astype(BF16)
    cm_w_in, cm_w_out, sb_w_qkv, sb_w_o, peer_w_q, peer_u, peer_v = map(
        to_bf, (cm_w_in, cm_w_out, sb_w_qkv, sb_w_o, peer_w_q, peer_u, peer_v))

    gate_p, gate_s, kp_rows, vp_rows, ks_rows, vs_rows = [], [], [], [], [], []
    dp = ds = None
    for layer in range(depth):
        li = layer // 2
        g_mix = norm_mix[layer]
        if layer == 0:
            hp, hs = rmsnorm(xp, g_mix), rmsnorm(xs, g_mix)
        else:
            xp, hp = rmsnorm(xp, g_mix, add=dp, emit_sum=True)
            xs, hs = rmsnorm(xs, g_mix, add=ds, emit_sum=True)
        if layer % 2 == 0:
            dg = cm_w_out.shape[1]
            zp = matmul(hp, cm_w_in[li], gelu=True)
            zs = matmul(hs, cm_w_in[li], gelu=True)
            pp, vnp = gating(zp, cm_norm[li], cm_w_s[li], cm_b_s[li])
            zs = jnp.pad(zs.reshape(bd, ts, 2 * dg), ((0, 0), (0, CHUNK - ts), (0, 0)))
            ps, vns = gating(zs.reshape(bd * CHUNK, 2 * dg), cm_norm[li], cm_w_s[li], cm_b_s[li])
            ps = ps.reshape(bd, CHUNK, dg)[:, :ts].reshape(bd * ts, dg)
            gate_p.append(vnp.reshape(b, t, dg)[:, t - CHUNK:])
            gate_s.append(vns.reshape(bd, CHUNK, dg)[:, :ts])
            xp = matmul(pp, cm_w_out[li], res=xp)
            xs = matmul(ps, cm_w_out[li], res=xs)
        else:
            w_qkv = sb_w_qkv[li]
            qp = matmul(hp, w_qkv, n=hd, col0=0, out_dtypes=(BF16,))
            kp, kpb = matmul(hp, w_qkv, n=hd, col0=hd, out_dtypes=(F32, BF16))
            vp, vpb = matmul(hp, w_qkv, n=hd, col0=2 * hd, out_dtypes=(F32, BF16))
            op = sb_attention_prompt(qp, kpb, vpb, sb_bias[li], batch=b, heads=heads)
            xp = matmul(op, sb_w_o[li], res=xp)
            qkv_s = matmul(hs, w_qkv).reshape(bd, ts, 3, hd)
            qs, kq, vq = qkv_s[:, :, 0], qkv_s[:, :, 1], qkv_s[:, :, 2]
            os_ = sb_attention_sample(qs, kq, vq, sb_bias[li], cache_k, cache_v, page_table,
                                      li, heads=heads)
            xs = matmul(os_.reshape(bd * ts, hd).astype(BF16), sb_w_o[li], res=xs)
            shape5 = lambda a, n0, n1: a.reshape(n0, n1, heads, hd // heads)
            kp_rows.append(shape5(kp, b, t))
            vp_rows.append(shape5(vp, b, t))
            ks_rows.append(shape5(kq, bd, ts))
            vs_rows.append(shape5(vq, bd, ts))
        g_ffn = norm_ffn[layer]
        dp = peer(rmsnorm(xp, g_ffn), peer_w_q[layer], peer_subkeys[layer],
                  peer_u[layer], peer_v[layer])
        ds = peer(rmsnorm(xs, g_ffn), peer_w_q[layer], peer_subkeys[layer],
                  peer_u[layer], peer_v[layer])
    y_prompt = rmsnorm(xp, norm_final, add=dp, out_dtype=F32).reshape(b, t, d)
    y_sample = rmsnorm(xs, norm_final, add=ds, out_dtype=F32).reshape(bd, ts, d)
    return (y_prompt, y_sample, jnp.stack(gate_p), jnp.stack(gate_s),
            jnp.stack(kp_rows), jnp.stack(vp_rows), jnp.stack(ks_rows), jnp.stack(vs_rows))
```

```python
import functools

import jax
import jax.numpy as jnp
import numpy as np
from jax import lax
from jax.experimental import pallas as pl
from jax.experimental.pallas import tpu as pltpu

EPS = 1e-6
CHUNK = 128
PEER_TOPK = 16
V7X_VMEM_LIMIT_BYTES = 56 * 1024 * 1024
LANES = 128
SUBLANES = 8
F32 = jnp.float32
BF16 = jnp.bfloat16
SQRT_HALF = float(np.sqrt(0.5))
NT_DIMS = (((1,), (1,)), ((), ()))


def _params(*sem):
    return pltpu.CompilerParams(dimension_semantics=sem,
                                vmem_limit_bytes=V7X_VMEM_LIMIT_BYTES)


def _gelu(x):
    return 0.5 * x * (1.0 + lax.erf(x * SQRT_HALF))


def _split_bf16(x):
    hi = x.astype(BF16)
    lo = (x - hi.astype(F32)).astype(BF16)
    return hi, lo


def _tile(n, pref):
    return pref if n % pref == 0 else n


def _rmsnorm_kernel(*refs, has_add, emit_sum):
    if has_add:
        x_ref, d_ref, g_ref, *outs = refs
        x = x_ref[...] + d_ref[...]
    else:
        x_ref, g_ref, *outs = refs
        x = x_ref[...]
    if emit_sum:
        outs[0][...] = x
    o_ref = outs[-1]
    ms = jnp.mean(x * x, axis=-1, keepdims=True)
    o_ref[...] = (x * lax.rsqrt(ms + EPS) * g_ref[...]).astype(o_ref.dtype)


def rmsnorm(x, g, *, add=None, emit_sum=False, out_dtype=BF16):
    m, d = x.shape
    tr = _tile(m, 256)
    row = pl.BlockSpec((tr, d), lambda i: (i, 0))
    ins = [x] + ([add] if add is not None else []) + [g.reshape(1, d)]
    in_specs = [row] * (len(ins) - 1) + [pl.BlockSpec((1, d), lambda i: (0, 0))]
    out_shape = [jax.ShapeDtypeStruct((m, d), out_dtype)]
    if emit_sum:
        out_shape = [jax.ShapeDtypeStruct((m, d), F32)] + out_shape
    res = pl.pallas_call(
        functools.partial(_rmsnorm_kernel, has_add=add is not None, emit_sum=emit_sum),
        grid=(m // tr,), in_specs=in_specs, out_specs=[row] * len(out_shape),
        out_shape=out_shape, compiler_params=_params("parallel"), name="rmsnorm")(*ins)
    return res if emit_sum else res[0]


def _matmul_kernel(*refs, has_res, gelu):
    a_ref, w_ref = refs[:2]
    outs = refs[3:] if has_res else refs[2:]
    acc = jnp.dot(a_ref[...], w_ref[...], preferred_element_type=F32)
    if has_res:
        acc = refs[2][...] + acc
    if gelu:
        acc = _gelu(acc)
    for o in outs:
        o[...] = acc.astype(o.dtype)


def matmul(a, w, layer, *, n=None, col0=0, res=None, gelu=False, out_dtypes=(F32,)):
    m, k = a.shape
    n = w.shape[2] if n is None else n
    tm = _tile(m, 1024)
    tn = _tile(n, 512 if (res is not None or len(out_dtypes) > 1) else 1024)
    assert col0 % tn == 0
    cb = col0 // tn
    ins = [a, w] + ([res] if res is not None else [])
    tile = pl.BlockSpec((tm, tn), lambda i, j: (i, j))
    in_specs = [pl.BlockSpec((tm, k), lambda i, j: (i, 0)),
                pl.BlockSpec((None, k, tn), lambda i, j: (layer, 0, j + cb))]
    if res is not None:
        in_specs.append(tile)
    outs = pl.pallas_call(
        functools.partial(_matmul_kernel, has_res=res is not None, gelu=gelu),
        grid=(m // tm, n // tn), in_specs=in_specs,
        out_specs=[tile] * len(out_dtypes),
        out_shape=[jax.ShapeDtypeStruct((m, n), dt) for dt in out_dtypes],
        compiler_params=_params("parallel", "parallel"), name="matmul")(*ins)
    return outs if len(out_dtypes) > 1 else outs[0]


def _gating_kernel(u_ref, v_ref, gn_ref, ws_ref, b_ref, p_ref, vn_ref, *, groups, gd):
    v = v_ref[...]
    ms = jnp.mean(v * v, axis=-1, keepdims=True)
    vn = v * lax.rsqrt(ms + EPS) * gn_ref[...]
    vn_ref[...] = vn
    t_i = lax.broadcasted_iota(jnp.int32, (CHUNK, CHUNK), 0)
    s_i = lax.broadcasted_iota(jnp.int32, (CHUNK, CHUNK), 1)
    causal = s_i <= t_i
    for g in range(groups):
        cols = slice(g * gd, (g + 1) * gd)
        wm = jnp.where(causal, ws_ref[g], 0.0).astype(BF16)
        s = jnp.dot(wm, vn[:, cols].astype(BF16), preferred_element_type=F32)
        s = s + b_ref[:, cols]
        p_ref[:, cols] = (u_ref[:, cols] * s).astype(p_ref.dtype)


def gating(z, g_norm, w_s, b_s):
    rows, dg2 = z.shape
    dg = dg2 // 2
    groups = w_s.shape[0]
    gd = dg // groups
    b_full = jnp.repeat(b_s.T, gd, axis=1)
    half = lambda c: pl.BlockSpec((CHUNK, dg), lambda i: (i, c))
    return pl.pallas_call(
        functools.partial(_gating_kernel, groups=groups, gd=gd),
        grid=(rows // CHUNK,),
        in_specs=[half(0), half(1),
                  pl.BlockSpec((1, dg), lambda i: (0, 0)),
                  pl.BlockSpec((groups, CHUNK, CHUNK), lambda i: (0, 0, 0)),
                  pl.BlockSpec((CHUNK, dg), lambda i: (0, 0))],
        out_specs=[half(0), half(0)],
        out_shape=[jax.ShapeDtypeStruct((rows, dg), BF16),
                   jax.ShapeDtypeStruct((rows, dg), F32)],
        compiler_params=_params("parallel"), name="gating",
    )(z, z, g_norm.reshape(1, dg), w_s, b_full)


def _sb_weights(z, tri, c, mask):
    lg = jnp.log(1.0 + jnp.exp(-jnp.abs(z)))
    l1m = jnp.minimum(-z, 0.0) - lg
    ls = z + l1m
    if mask is not None:
        l1m = jnp.where(mask, l1m, 0.0)
    hi, lo = _split_bf16(l1m)
    excl = (jnp.dot(hi, tri, preferred_element_type=F32)
            + jnp.dot(lo, tri, preferred_element_type=F32))
    a = jnp.exp(ls + excl + c)
    if mask is not None:
        a = jnp.where(mask, a, 0.0)
    return a, l1m


def _sb_prompt_kernel(bias_ref, q_ref, k_ref, v_ref, o_ref, *, t, dh, hps, scale):
    h0 = pl.program_id(1) * hps
    qi = pl.program_id(2)
    r_i = lax.broadcasted_iota(jnp.int32, (t, t), 0)
    c_i = lax.broadcasted_iota(jnp.int32, (t, t), 1)
    tri = jnp.where(r_i > c_i, 1.0, 0.0).astype(BF16)
    cols = [slice(x * dh, (x + 1) * dh) for x in range(hps)]
    qs = [(q_ref[:, cs].astype(F32) * scale).astype(BF16) for cs in cols]
    bias = [bias_ref[h0 + x] for x in range(hps)]

    def step(kb, carry, mask):
        ks = pl.multiple_of(kb * t, t)
        out = []
        for x, (c, acc) in enumerate(carry):
            kblk = k_ref[pl.ds(ks, t), cols[x]]
            vblk = v_ref[pl.ds(ks, t), cols[x]]
            z = lax.dot_general(qs[x], kblk, NT_DIMS, preferred_element_type=F32) + bias[x]
            a, l1m = _sb_weights(z, tri, c, mask)
            acc = acc + jnp.dot(a.astype(BF16), vblk, preferred_element_type=F32)
            out.append((c + jnp.sum(l1m, axis=1, keepdims=True), acc))
        return tuple(out)

    zero = (jnp.zeros((t, 1), F32), jnp.zeros((t, dh), F32))
    carry = step(qi, (zero,) * hps, c_i < r_i)
    carry = lax.fori_loop(
        0, qi // 2,
        lambda i, cr: step(qi - 2 - 2 * i, step(qi - 1 - 2 * i, cr, None), None), carry)
    carry = lax.cond(qi % 2 == 1, lambda cr: step(0, cr, None), lambda cr: cr, carry)
    for x, (_, acc) in enumerate(carry):
        o_ref[:, cols[x]] = acc.astype(o_ref.dtype)


def sb_attention_prompt(q, k, v, bias, *, batch, heads):
    m, hd = q.shape
    t = m // batch
    dh = hd // heads
    tq = _tile(t, 256)
    nq = t // tq
    hps = 2 if heads % 2 == 0 else 1
    qspec = pl.BlockSpec((tq, hps * dh), lambda b, h, i: (b * nq + i, h))
    kvspec = pl.BlockSpec((t, hps * dh), lambda b, h, i: (b, h))
    return pl.pallas_call(
        functools.partial(_sb_prompt_kernel, t=tq, dh=dh, hps=hps, scale=dh ** -0.5),
        grid=(batch, heads // hps, nq),
        in_specs=[pl.BlockSpec(memory_space=pltpu.SMEM), qspec, kvspec, kvspec],
        out_specs=qspec,
        out_shape=jax.ShapeDtypeStruct((m, hd), BF16),
        compiler_params=_params("parallel", "parallel", "parallel"), name="sb_prompt",
    )(bias.astype(F32), q, k, v)


def _sb_sample_kernel(pt_ref, bias_ref, q_ref, e8_ref, e8t_ref, knew_ref, vnew_ref, *rest,
                      tq, page, pps, scale):
    del pt_ref
    kp_refs, vp_refs = rest[:pps], rest[pps:2 * pps]
    o_ref, c_ref, acc_ref = rest[2 * pps:]
    j = pl.program_id(1)
    ng, dh = knew_ref.shape[1], knew_ref.shape[3]
    gr = SUBLANES * tq
    lanes = page * SUBLANES
    j_i = lax.broadcasted_iota(jnp.int32, (page, page), 0)
    s_i = lax.broadcasted_iota(jnp.int32, (page, page), 1)
    tri = jnp.where(j_i > s_i, 1.0, 0.0).astype(BF16)
    own = (lax.broadcasted_iota(jnp.int32, (gr, lanes), 1) % SUBLANES
           == lax.broadcasted_iota(jnp.int32, (gr, lanes), 0) // tq)

    def blocks(kv_refs, mask, c, acc):
        nr = ng * gr
        for k_ref, v_ref in kv_refs:
            ss = []
            for g in range(ng):
                kg = k_ref[:, g].reshape(lanes, dh).astype(BF16)
                s = lax.dot_general(q_ref[g * gr:(g + 1) * gr, :], kg, NT_DIMS,
                                    preferred_element_type=F32)
                ss.append(jnp.where(own, s, 0.0))
            hi_lo = jnp.concatenate(_split_bf16(jnp.concatenate(ss, axis=0)), axis=0)
            z2 = jnp.dot(hi_lo, e8t_ref[...], preferred_element_type=F32)
            z = (z2[:nr] + z2[nr:]) * scale + bias_ref[...]
            a, l1m = _sb_weights(z, tri, c, mask)
            c = c + jnp.sum(l1m, axis=1, keepdims=True)
            spread = jnp.dot(a.astype(BF16), e8_ref[...], preferred_element_type=F32)
            outs = []
            for g in range(ng):
                ae = jnp.where(own, spread[g * gr:(g + 1) * gr], 0.0).astype(BF16)
                vg = v_ref[:, g].reshape(lanes, dh).astype(BF16)
                outs.append(jnp.dot(ae, vg, preferred_element_type=F32))
            acc = acc + jnp.concatenate(outs, axis=0)
        c_ref[...] = c
        acc_ref[...] = acc

    @pl.when(j == 0)
    def _():
        key = lax.broadcasted_iota(jnp.int32, (ng * gr, page), 1)
        qry = lax.broadcasted_iota(jnp.int32, (ng * gr, page), 0) % tq
        blocks([(knew_ref, vnew_ref)], key < qry, jnp.zeros(c_ref.shape, F32),
               jnp.zeros(acc_ref.shape, F32))

    @pl.when(j > 0)
    def _():
        blocks([(kp_refs[p], vp_refs[p]) for p in reversed(range(pps))], None,
               c_ref[...], acc_ref[...])

    @pl.when(j == pl.num_programs(1) - 1)
    def _():
        o_ref[...] = acc_ref[...]


def sb_attention_sample(q, k_new, v_new, bias, cache_k, cache_v, page_table, li, *, heads):
    bd, tq, hd = q.shape
    dh = hd // heads
    page = cache_k.shape[2]
    n_pages = page_table.shape[1]
    assert heads % SUBLANES == 0 and tq <= page
    ng = heads // SUBLANES
    nr = heads * tq
    pps = next(p for p in (4, 2, 1) if n_pages % p == 0)
    n_groups = n_pages // pps
    q_rows = q.reshape(bd, tq, heads, dh).transpose(0, 2, 1, 3).reshape(bd, nr, dh).astype(BF16)
    bias_rows = jnp.broadcast_to(jnp.repeat(bias.astype(F32), tq)[:, None], (nr, page))
    e8 = np.kron(np.eye(page, dtype=np.float32), np.ones((1, SUBLANES), np.float32))
    e8, e8t = jnp.asarray(e8, BF16), jnp.asarray(e8.T, BF16)
    grouped = lambda c: c.reshape(*c.shape[:-2], ng, SUBLANES, dh)
    new = lambda x: grouped(jnp.pad(x.reshape(bd, tq, heads, dh),
                                    ((0, 0), (0, page - tq), (0, 0), (0, 0))))

    def page_spec(p):
        def index_map(b, j, pt):
            return (li, pt[b, (n_groups - jnp.maximum(j, 1)) * pps + p], 0, 0, 0, 0)
        return pl.BlockSpec((None, None, page, ng, SUBLANES, dh), index_map)

    const = lambda shape: pl.BlockSpec(shape, lambda b, j, pt: (0,) * len(shape))
    per_b = lambda shape: pl.BlockSpec((None,) + shape, lambda b, j, pt: (b,) + (0,) * len(shape))
    grid_spec = pltpu.PrefetchScalarGridSpec(
        num_scalar_prefetch=1, grid=(bd, n_groups + 1),
        in_specs=[const((nr, page)), per_b((nr, dh)),
                  const((page, page * SUBLANES)), const((page * SUBLANES, page)),
                  per_b((page, ng, SUBLANES, dh)), per_b((page, ng, SUBLANES, dh))]
                 + [page_spec(p) for p in range(pps)] * 2,
        out_specs=per_b((nr, dh)),
        scratch_shapes=[pltpu.VMEM((nr, 1), F32), pltpu.VMEM((nr, dh), F32)])
    o = pl.pallas_call(
        functools.partial(_sb_sample_kernel, tq=tq, page=page, pps=pps, scale=dh ** -0.5),
        grid_spec=grid_spec, out_shape=jax.ShapeDtypeStruct((bd, nr, dh), F32),
        compiler_params=_params("parallel", "arbitrary"), name="sb_sample",
    )(page_table, bias_rows, q_rows, e8, e8t, new(k_new), new(v_new),
      *([grouped(cache_k)] * pps), *([grouped(cache_v)] * pps))
    return o.reshape(bd, heads, tq, dh).transpose(0, 2, 1, 3).reshape(bd, tq, hd)


def _topk_rows(s, k):
    r = s.shape[0]
    rows = lax.broadcasted_iota(jnp.int32, s.shape, 0)
    vals, ids = [], []
    for _ in range(k):
        m = jnp.max(s, axis=0, keepdims=True)
        idx = jnp.min(jnp.where(s == m, rows, r), axis=0, keepdims=True)
        vals.append(m)
        ids.append(idx)
        s = jnp.where(rows == idx, -jnp.inf, s)
    return jnp.concatenate(vals, axis=0), jnp.concatenate(ids, axis=0)


def _peer_route_kernel(q_ref, sub_ref, e_ref, g_ref, *, nk, topk):
    dkh = sub_ref.shape[-1]
    q = q_ref[...]
    tops = []
    for p in range(2):
        qh, ql = _split_bf16(q[:, p * dkh:(p + 1) * dkh])
        sh, sl = _split_bf16(sub_ref[p])
        nt = lambda a, b: lax.dot_general(a, b, NT_DIMS, preferred_element_type=F32)
        s = nt(sh, qh) + nt(sh, ql) + nt(sl, qh)
        tops.append(_topk_rows(s, topk))
    (s1, i1), (s2, i2) = tops
    width = [topk // (i + 1) for i in range(topk)]
    cand = jnp.concatenate([s1[i:i + 1] + s2[:width[i]] for i in range(topk)], axis=0)
    cidx = jnp.concatenate([i1[i:i + 1] * nk + i2[:width[i]] for i in range(topk)], axis=0)
    rows = lax.broadcasted_iota(jnp.int32, cand.shape, 0)
    top, sel = _topk_rows(cand, topk)
    e_ref[...] = jnp.concatenate(
        [jnp.max(jnp.where(rows == sel[i:i + 1], cidx, -1), axis=0, keepdims=True)
         for i in range(topk)], axis=0)
    ex = jnp.exp(top - jnp.max(top, axis=0, keepdims=True))
    g_ref[...] = ex / jnp.sum(ex, axis=0, keepdims=True)


def peer_route(q, subkeys, layer):
    m, _ = q.shape
    _, heads, _, nk, dkh = subkeys.shape
    tt = _tile(m, 256)
    out = pl.BlockSpec((PEER_TOPK, tt), lambda i, h: (h, i))
    return pl.pallas_call(
        functools.partial(_peer_route_kernel, nk=nk, topk=PEER_TOPK),
        grid=(m // tt, heads),
        in_specs=[pl.BlockSpec((tt, 2 * dkh), lambda i, h: (i, h)),
                  pl.BlockSpec((None, None, 2, nk, dkh), lambda i, h: (layer, h, 0, 0, 0))],
        out_specs=[out, out],
        out_shape=[jax.ShapeDtypeStruct((heads * PEER_TOPK, m), jnp.int32),
                   jax.ShapeDtypeStruct((heads * PEER_TOPK, m), F32)],
        compiler_params=_params("parallel", "parallel"), name="peer_route")(q, subkeys)


def _peer_w_kernel(e_ref, g_ref, w_ref, *, nk):
    tb, hk = e_ref.shape
    key = lax.broadcasted_iota(jnp.int32, (nk, hk), 0)

    def body(t, _):
        e = e_ref[pl.ds(t, 1), :]
        g = g_ref[pl.ds(t, 1), :]
        a = jnp.where(key == e // nk, g, 0.0).astype(BF16)
        b = jnp.where(key == e % nk, 1.0, 0.0).astype(BF16)
        w = lax.dot_general(a, b, NT_DIMS, preferred_element_type=F32)
        w_ref[:, pl.ds(t, 1)] = w.reshape(nk // SUBLANES, 1, SUBLANES, nk)
        return 0

    lax.fori_loop(0, tb, body, 0, unroll=8)


def peer_gate_matrix(eidx, gate, nk):
    m, hk = eidx.shape
    tb = _tile(m, 128)
    row = pl.BlockSpec((tb, hk), lambda i: (i, 0))
    w = pl.pallas_call(
        functools.partial(_peer_w_kernel, nk=nk),
        grid=(m // tb,), in_specs=[row, row],
        out_specs=pl.BlockSpec((nk // SUBLANES, tb, SUBLANES, nk), lambda i: (0, i, 0, 0)),
        out_shape=jax.ShapeDtypeStruct((nk // SUBLANES, m, SUBLANES, nk), F32),
        compiler_params=_params("parallel"), name="peer_w")(eidx, gate)
    return w.reshape(nk // SUBLANES, m * SUBLANES, nk)


def _peer_dense_kernel(h_ref, u_ref, v_ref, w_ref, o_ref, *, tn):
    j = pl.program_id(1)
    tm = h_ref.shape[0]
    nk = w_ref.shape[1]
    act = _gelu(lax.dot_general(h_ref[...], u_ref[...], NT_DIMS, preferred_element_type=F32))
    p = jnp.concatenate(
        [w_ref[pl.ds(a, tm, stride=SUBLANES), :] * act[:, a * nk:(a + 1) * nk]
         for a in range(SUBLANES)], axis=1).astype(BF16)

    @pl.when(j == 0)
    def _():
        o_ref[...] = jnp.zeros_like(o_ref)

    for c in range(0, o_ref.shape[1], tn):
        o_ref[:, c:c + tn] += jnp.dot(p, v_ref[:, c:c + tn], preferred_element_type=F32)


def peer_dense(h, u, v, w, layer):
    m, d = h.shape
    e = u.shape[1]
    nk = w.shape[2]
    tm = _tile(m, 512)
    te = SUBLANES * nk
    assert e == nk * nk and nk % SUBLANES == 0
    once = pl.Buffered(1)
    return pl.pallas_call(
        functools.partial(_peer_dense_kernel, tn=_tile(d, 1024)), grid=(m // tm, e // te),
        in_specs=[pl.BlockSpec((tm, d), lambda i, j: (i, 0), pipeline_mode=once),
                  pl.BlockSpec((None, te, d), lambda i, j: (layer, j, 0)),
                  pl.BlockSpec((None, te, d), lambda i, j: (layer, j, 0)),
                  pl.BlockSpec((None, tm * SUBLANES, nk), lambda i, j: (j, i, 0))],
        out_specs=pl.BlockSpec((tm, d), lambda i, j: (i, 0), pipeline_mode=once),
        out_shape=jax.ShapeDtypeStruct((m, d), F32),
        compiler_params=_params("parallel", "arbitrary"), name="peer_dense")(h, u, v, w)


def peer(h, w_q, subkeys, u, v, layer):
    m = h.shape[0]
    nk = subkeys.shape[3]
    q = matmul(h, w_q, layer)
    mp = -(-m // LANES) * LANES
    e_t, g_t = peer_route(jnp.pad(q, ((0, mp - m), (0, 0))), subkeys, layer)
    w = peer_gate_matrix(e_t.T[:m], g_t.T[:m], nk)
    return peer_dense(h, u, v, w, layer)


def kernel(x_prompt, x_sample, cache_k, cache_v, page_table, norm_mix, norm_ffn, norm_final,
           cm_w_in, cm_norm, cm_w_s, cm_b_s, cm_w_out, sb_w_qkv, sb_bias, sb_w_o,
           peer_w_q, peer_subkeys, peer_u, peer_v):
    b, t, d = x_prompt.shape
    bd, ts, _ = x_sample.shape
    depth = norm_mix.shape[0]
    heads = cache_k.shape[3]
    hd = heads * cache_k.shape[4]
    xp = x_prompt.reshape(b * t, d)
    xs = x_sample.reshape(bd * ts, d)
    assert t % CHUNK == 0 and ts <= CHUNK
    to_bf = lambda a: a.astype(BF16)
    cm_w_in, cm_w_out, sb_w_qkv, sb_w_o, peer_w_q, peer_u, peer_v = map(
        to_bf, (cm_w_in, cm_w_out, sb_w_qkv, sb_w_o, peer_w_q, peer_u, peer_v))

    gate_p, gate_s, kp_rows, vp_rows, ks_rows, vs_rows = [], [], [], [], [], []
    dp = ds = None
    for layer in range(depth):
        li = layer // 2
        g_mix = norm_mix[layer]
        if layer == 0:
            hp, hs = rmsnorm(xp, g_mix), rmsnorm(xs, g_mix)
        else:
            xp, hp = rmsnorm(xp, g_mix, add=dp, emit_sum=True)
            xs, hs = rmsnorm(xs, g_mix, add=ds, emit_sum=True)
        if layer % 2 == 0:
            dg = cm_w_out.shape[1]
            zp = matmul(hp, cm_w_in, li, gelu=True)
            zs = matmul(hs, cm_w_in, li, gelu=True)
            pp, vnp = gating(zp, cm_norm[li], cm_w_s[li], cm_b_s[li])
            zs = jnp.pad(zs.reshape(bd, ts, 2 * dg), ((0, 0), (0, CHUNK - ts), (0, 0)))
            ps, vns = gating(zs.reshape(bd * CHUNK, 2 * dg), cm_norm[li], cm_w_s[li], cm_b_s[li])
            ps = ps.reshape(bd, CHUNK, dg)[:, :ts].reshape(bd * ts, dg)
            gate_p.append(vnp.reshape(b, t, dg)[:, t - CHUNK:])
            gate_s.append(vns.reshape(bd, CHUNK, dg)[:, :ts])
            xp = matmul(pp, cm_w_out, li, res=xp)
            xs = matmul(ps, cm_w_out, li, res=xs)
        else:
            qp = matmul(hp, sb_w_qkv, li, n=hd, col0=0, out_dtypes=(BF16,))
            kp, kpb = matmul(hp, sb_w_qkv, li, n=hd, col0=hd, out_dtypes=(F32, BF16))
            vp, vpb = matmul(hp, sb_w_qkv, li, n=hd, col0=2 * hd, out_dtypes=(F32, BF16))
            op = sb_attention_prompt(qp, kpb, vpb, sb_bias[li], batch=b, heads=heads)
            xp = matmul(op, sb_w_o, li, res=xp)
            qkv_s = matmul(hs, sb_w_qkv, li).reshape(bd, ts, 3, hd)
            qs, kq, vq = qkv_s[:, :, 0], qkv_s[:, :, 1], qkv_s[:, :, 2]
            os_ = sb_attention_sample(qs, kq, vq, sb_bias[li], cache_k, cache_v, page_table,
                                      li, heads=heads)
            xs = matmul(os_.reshape(bd * ts, hd).astype(BF16), sb_w_o, li, res=xs)
            shape5 = lambda a, n0, n1: a.reshape(n0, n1, heads, hd // heads)
            kp_rows.append(shape5(kp, b, t))
            vp_rows.append(shape5(vp, b, t))
            ks_rows.append(shape5(kq, bd, ts))
            vs_rows.append(shape5(vq, bd, ts))
        g_ffn = norm_ffn[layer]
        dp = peer(rmsnorm(xp, g_ffn), peer_w_q, peer_subkeys, peer_u, peer_v, layer)
        ds = peer(rmsnorm(xs, g_ffn), peer_w_q, peer_subkeys, peer_u, peer_v, layer)
    y_prompt = rmsnorm(xp, norm_final, add=dp, out_dtype=F32).reshape(b, t, d)
    y_sample = rmsnorm(xs, norm_final, add=ds, out_dtype=F32).reshape(bd, ts, d)
    return (y_prompt, y_sample, jnp.stack(gate_p), jnp.stack(gate_s),
            jnp.stack(kp_rows), jnp.stack(vp_rows), jnp.stack(ks_rows), jnp.stack(vs_rows))
```

```python
import functools

import jax
import jax.numpy as jnp
import numpy as np
from jax import lax
from jax.experimental import pallas as pl
from jax.experimental.pallas import tpu as pltpu

EPS = 1e-6
CHUNK = 128
PEER_TOPK = 16
V7X_VMEM_LIMIT_BYTES = 56 * 1024 * 1024
LANES = 128
SUBLANES = 8
F32 = jnp.float32
BF16 = jnp.bfloat16
SQRT_HALF = float(np.sqrt(0.5))
NT_DIMS = (((1,), (1,)), ((), ()))


def _params(*sem):
    return pltpu.CompilerParams(dimension_semantics=sem,
                                vmem_limit_bytes=V7X_VMEM_LIMIT_BYTES)


def _gelu(x):
    return 0.5 * x * (1.0 + lax.erf(x * SQRT_HALF))


def _split_bf16(x):
    hi = x.astype(BF16)
    lo = (x - hi.astype(F32)).astype(BF16)
    return hi, lo


def _tile(n, pref):
    return pref if n % pref == 0 else n


def _rmsnorm_kernel(*refs, has_add, emit_sum):
    if has_add:
        x_ref, d_ref, g_ref, *outs = refs
        x = x_ref[...] + d_ref[...]
    else:
        x_ref, g_ref, *outs = refs
        x = x_ref[...]
    if emit_sum:
        outs[0][...] = x
    o_ref = outs[-1]
    ms = jnp.mean(x * x, axis=-1, keepdims=True)
    o_ref[...] = (x * lax.rsqrt(ms + EPS) * g_ref[...]).astype(o_ref.dtype)


def rmsnorm(x, g, *, add=None, emit_sum=False, out_dtype=BF16):
    m, d = x.shape
    tr = _tile(m, 256)
    row = pl.BlockSpec((tr, d), lambda i: (i, 0))
    ins = [x] + ([add] if add is not None else []) + [g.reshape(1, d)]
    in_specs = [row] * (len(ins) - 1) + [pl.BlockSpec((1, d), lambda i: (0, 0))]
    out_shape = [jax.ShapeDtypeStruct((m, d), out_dtype)]
    if emit_sum:
        out_shape = [jax.ShapeDtypeStruct((m, d), F32)] + out_shape
    res = pl.pallas_call(
        functools.partial(_rmsnorm_kernel, has_add=add is not None, emit_sum=emit_sum),
        grid=(m // tr,), in_specs=in_specs, out_specs=[row] * len(out_shape),
        out_shape=out_shape, compiler_params=_params("parallel"), name="rmsnorm")(*ins)
    return res if emit_sum else res[0]


def _matmul_kernel(*refs, has_res, gelu):
    a_ref, w_ref = refs[:2]
    outs = refs[3:] if has_res else refs[2:]
    acc = jnp.dot(a_ref[...], w_ref[...], preferred_element_type=F32)
    if has_res:
        acc = refs[2][...] + acc
    if gelu:
        acc = _gelu(acc)
    for o in outs:
        o[...] = acc.astype(o.dtype)


def matmul(a, w, layer, *, n=None, col0=0, res=None, gelu=False, out_dtypes=(F32,)):
    m, k = a.shape
    n = w.shape[2] if n is None else n
    tm = _tile(m, 1024)
    tn = _tile(n, 512 if (res is not None or len(out_dtypes) > 1) else 1024)
    assert col0 % tn == 0
    cb = col0 // tn
    ins = [a, w] + ([res] if res is not None else [])
    tile = pl.BlockSpec((tm, tn), lambda i, j: (i, j))
    in_specs = [pl.BlockSpec((tm, k), lambda i, j: (i, 0)),
                pl.BlockSpec((None, k, tn), lambda i, j: (layer, 0, j + cb))]
    if res is not None:
        in_specs.append(tile)
    outs = pl.pallas_call(
        functools.partial(_matmul_kernel, has_res=res is not None, gelu=gelu),
        grid=(m // tm, n // tn), in_specs=in_specs,
        out_specs=[tile] * len(out_dtypes),
        out_shape=[jax.ShapeDtypeStruct((m, n), dt) for dt in out_dtypes],
        compiler_params=_params("parallel", "parallel"), name="matmul")(*ins)
    return outs if len(out_dtypes) > 1 else outs[0]


def _gating_kernel(u_ref, v_ref, gn_ref, ws_ref, b_ref, p_ref, vn_ref, *, groups, gd):
    v = v_ref[...]
    ms = jnp.mean(v * v, axis=-1, keepdims=True)
    vn = v * lax.rsqrt(ms + EPS) * gn_ref[...]
    vn_ref[...] = vn
    t_i = lax.broadcasted_iota(jnp.int32, (CHUNK, CHUNK), 0)
    s_i = lax.broadcasted_iota(jnp.int32, (CHUNK, CHUNK), 1)
    causal = s_i <= t_i
    for g in range(groups):
        cols = slice(g * gd, (g + 1) * gd)
        wm = jnp.where(causal, ws_ref[g], 0.0).astype(BF16)
        s = jnp.dot(wm, vn[:, cols].astype(BF16), preferred_element_type=F32)
        s = s + b_ref[:, cols]
        p_ref[:, cols] = (u_ref[:, cols] * s).astype(p_ref.dtype)


def gating(z, g_norm, w_s, b_s):
    rows, dg2 = z.shape
    dg = dg2 // 2
    groups = w_s.shape[0]
    gd = dg // groups
    b_full = jnp.repeat(b_s.T, gd, axis=1)
    half = lambda c: pl.BlockSpec((CHUNK, dg), lambda i: (i, c))
    return pl.pallas_call(
        functools.partial(_gating_kernel, groups=groups, gd=gd),
        grid=(rows // CHUNK,),
        in_specs=[half(0), half(1),
                  pl.BlockSpec((1, dg), lambda i: (0, 0)),
                  pl.BlockSpec((groups, CHUNK, CHUNK), lambda i: (0, 0, 0)),
                  pl.BlockSpec((CHUNK, dg), lambda i: (0, 0))],
        out_specs=[half(0), half(0)],
        out_shape=[jax.ShapeDtypeStruct((rows, dg), BF16),
                   jax.ShapeDtypeStruct((rows, dg), F32)],
        compiler_params=_params("parallel"), name="gating",
    )(z, z, g_norm.reshape(1, dg), w_s, b_full)


def _sb_log_weights(z, tri, mask):
    lg = jnp.log(1.0 + jnp.exp(-jnp.abs(z)))
    l1m = jnp.minimum(-z, 0.0) - lg
    ls = z + l1m
    if mask is not None:
        l1m = jnp.where(mask, l1m, 0.0)
    hi, lo = _split_bf16(l1m)
    excl = (jnp.dot(hi, tri, preferred_element_type=F32)
            + jnp.dot(lo, tri, preferred_element_type=F32))
    pre = ls + excl
    if mask is not None:
        pre = jnp.where(mask, pre, -jnp.inf)
    return pre, jnp.sum(l1m, axis=1, keepdims=True)


def _sb_prompt_kernel(bias_ref, q_ref, k_ref, v_ref, o_ref, *, t, dh, hps, scale):
    h0 = pl.program_id(1) * hps
    qi = pl.program_id(2)
    r_i = lax.broadcasted_iota(jnp.int32, (t, t), 0)
    c_i = lax.broadcasted_iota(jnp.int32, (t, t), 1)
    tri = jnp.where(r_i > c_i, 1.0, 0.0).astype(BF16)
    cols = [slice(x * dh, (x + 1) * dh) for x in range(hps)]
    qs = [(q_ref[:, cs].astype(F32) * scale).astype(BF16) for cs in cols]
    bias = [bias_ref[h0 + x] for x in range(hps)]

    def log_weights(kb, mask):
        ks = pl.multiple_of(kb * t, t)
        out = []
        for x in range(hps):
            z = lax.dot_general(qs[x], k_ref[pl.ds(ks, t), cols[x]], NT_DIMS,
                                preferred_element_type=F32) + bias[x]
            out.append(_sb_log_weights(z, tri, mask))
        return tuple(out)

    def accumulate(kb, logw, carry):
        ks = pl.multiple_of(kb * t, t)
        out = []
        for x, ((pre, rowsum), (c, acc)) in enumerate(zip(logw, carry)):
            a = jnp.exp(pre + c).astype(BF16)
            acc = acc + jnp.dot(a, v_ref[pl.ds(ks, t), cols[x]], preferred_element_type=F32)
            out.append((c + rowsum, acc))
        return tuple(out)

    def body(i, state):
        logw, carry = state
        return log_weights(qi - 1 - i, None), accumulate(qi - i, logw, carry)

    zero = (jnp.zeros((t, 1), F32), jnp.zeros((t, dh), F32))
    logw, carry = lax.fori_loop(0, qi, body, (log_weights(qi, c_i < r_i), (zero,) * hps))
    carry = accumulate(0, logw, carry)
    for x, (_, acc) in enumerate(carry):
        o_ref[:, cols[x]] = acc.astype(o_ref.dtype)


def sb_attention_prompt(q, k, v, bias, *, batch, heads):
    m, hd = q.shape
    t = m // batch
    dh = hd // heads
    tq = _tile(t, 256)
    nq = t // tq
    hps = next(n for n in (4, 2, 1) if heads % n == 0)
    qspec = pl.BlockSpec((tq, hps * dh), lambda b, h, i: (b * nq + i, h))
    kvspec = pl.BlockSpec((t, hps * dh), lambda b, h, i: (b, h))
    return pl.pallas_call(
        functools.partial(_sb_prompt_kernel, t=tq, dh=dh, hps=hps, scale=dh ** -0.5),
        grid=(batch, heads // hps, nq),
        in_specs=[pl.BlockSpec(memory_space=pltpu.SMEM), qspec, kvspec, kvspec],
        out_specs=qspec,
        out_shape=jax.ShapeDtypeStruct((m, hd), BF16),
        compiler_params=_params("parallel", "parallel", "parallel"), name="sb_prompt",
    )(bias.astype(F32), q, k, v)


def _sb_sample_kernel(pt_ref, bias_ref, q_ref, e8_ref, e8t_ref, knew_ref, vnew_ref, *rest,
                      tq, page, pps, scale):
    del pt_ref
    kp_refs, vp_refs = rest[:pps], rest[pps:2 * pps]
    o_ref, c_ref, acc_ref = rest[2 * pps:]
    j = pl.program_id(1)
    ng, dh = knew_ref.shape[1], knew_ref.shape[3]
    gr = SUBLANES * tq
    lanes = page * SUBLANES
    j_i = lax.broadcasted_iota(jnp.int32, (page, page), 0)
    s_i = lax.broadcasted_iota(jnp.int32, (page, page), 1)
    tri = jnp.where(j_i > s_i, 1.0, 0.0).astype(BF16)
    own = (lax.broadcasted_iota(jnp.int32, (gr, lanes), 1) % SUBLANES
           == lax.broadcasted_iota(jnp.int32, (gr, lanes), 0) // tq)

    def blocks(kv_refs, mask, c, acc):
        nr = ng * gr
        for k_ref, v_ref in kv_refs:
            ss = []
            for g in range(ng):
                kg = k_ref[:, g].reshape(lanes, dh).astype(BF16)
                s = lax.dot_general(q_ref[g * gr:(g + 1) * gr, :], kg, NT_DIMS,
                                    preferred_element_type=F32)
                ss.append(jnp.where(own, s, 0.0))
            hi_lo = jnp.concatenate(_split_bf16(jnp.concatenate(ss, axis=0)), axis=0)
            z2 = jnp.dot(hi_lo, e8t_ref[...], preferred_element_type=F32)
            z = (z2[:nr] + z2[nr:]) * scale + bias_ref[...]
            pre, rowsum = _sb_log_weights(z, tri, mask)
            a = jnp.exp(pre + c)
            c = c + rowsum
            spread = jnp.dot(a.astype(BF16), e8_ref[...], preferred_element_type=F32)
            outs = []
            for g in range(ng):
                ae = jnp.where(own, spread[g * gr:(g + 1) * gr], 0.0).astype(BF16)
                vg = v_ref[:, g].reshape(lanes, dh).astype(BF16)
                outs.append(jnp.dot(ae, vg, preferred_element_type=F32))
            acc = acc + jnp.concatenate(outs, axis=0)
        c_ref[...] = c
        acc_ref[...] = acc

    @pl.when(j == 0)
    def _():
        key = lax.broadcasted_iota(jnp.int32, (ng * gr, page), 1)
        qry = lax.broadcasted_iota(jnp.int32, (ng * gr, page), 0) % tq
        blocks([(knew_ref, vnew_ref)], key < qry, jnp.zeros(c_ref.shape, F32),
               jnp.zeros(acc_ref.shape, F32))

    @pl.when(j > 0)
    def _():
        blocks([(kp_refs[p], vp_refs[p]) for p in reversed(range(pps))], None,
               c_ref[...], acc_ref[...])

    @pl.when(j == pl.num_programs(1) - 1)
    def _():
        o_ref[...] = acc_ref[...]


def sb_attention_sample(q, k_new, v_new, bias, cache_k, cache_v, page_table, li, *, heads):
    bd, tq, hd = q.shape
    dh = hd // heads
    page = cache_k.shape[2]
    n_pages = page_table.shape[1]
    assert heads % SUBLANES == 0 and tq <= page
    ng = heads // SUBLANES
    nr = heads * tq
    pps = next(p for p in (4, 2, 1) if n_pages % p == 0)
    n_groups = n_pages // pps
    q_rows = q.reshape(bd, tq, heads, dh).transpose(0, 2, 1, 3).reshape(bd, nr, dh).astype(BF16)
    bias_rows = jnp.broadcast_to(jnp.repeat(bias.astype(F32), tq)[:, None], (nr, page))
    e8 = np.kron(np.eye(page, dtype=np.float32), np.ones((1, SUBLANES), np.float32))
    e8, e8t = jnp.asarray(e8, BF16), jnp.asarray(e8.T, BF16)
    grouped = lambda c: c.reshape(*c.shape[:-2], ng, SUBLANES, dh)
    new = lambda x: grouped(jnp.pad(x.reshape(bd, tq, heads, dh),
                                    ((0, 0), (0, page - tq), (0, 0), (0, 0))))

    def page_spec(p):
        def index_map(b, j, pt):
            return (li, pt[b, (n_groups - jnp.maximum(j, 1)) * pps + p], 0, 0, 0, 0)
        return pl.BlockSpec((None, None, page, ng, SUBLANES, dh), index_map)

    const = lambda shape: pl.BlockSpec(shape, lambda b, j, pt: (0,) * len(shape))
    per_b = lambda shape: pl.BlockSpec((None,) + shape, lambda b, j, pt: (b,) + (0,) * len(shape))
    grid_spec = pltpu.PrefetchScalarGridSpec(
        num_scalar_prefetch=1, grid=(bd, n_groups + 1),
        in_specs=[const((nr, page)), per_b((nr, dh)),
                  const((page, page * SUBLANES)), const((page * SUBLANES, page)),
                  per_b((page, ng, SUBLANES, dh)), per_b((page, ng, SUBLANES, dh))]
                 + [page_spec(p) for p in range(pps)] * 2,
        out_specs=per_b((nr, dh)),
        scratch_shapes=[pltpu.VMEM((nr, 1), F32), pltpu.VMEM((nr, dh), F32)])
    o = pl.pallas_call(
        functools.partial(_sb_sample_kernel, tq=tq, page=page, pps=pps, scale=dh ** -0.5),
        grid_spec=grid_spec, out_shape=jax.ShapeDtypeStruct((bd, nr, dh), F32),
        compiler_params=_params("parallel", "arbitrary"), name="sb_sample",
    )(page_table, bias_rows, q_rows, e8, e8t, new(k_new), new(v_new),
      *([grouped(cache_k)] * pps), *([grouped(cache_v)] * pps))
    return o.reshape(bd, heads, tq, dh).transpose(0, 2, 1, 3).reshape(bd, tq, hd)


def _topk_rows(s, k):
    r = s.shape[0]
    rows = lax.broadcasted_iota(jnp.int32, s.shape, 0)
    vals, ids = [], []
    for _ in range(k):
        m = jnp.max(s, axis=0, keepdims=True)
        idx = jnp.min(jnp.where(s == m, rows, r), axis=0, keepdims=True)
        vals.append(m)
        ids.append(idx)
        s = jnp.where(rows == idx, -jnp.inf, s)
    return jnp.concatenate(vals, axis=0), jnp.concatenate(ids, axis=0)


def _peer_route_kernel(q_ref, sub_ref, e_ref, g_ref, *, nk, topk):
    dkh = sub_ref.shape[-1]
    q = q_ref[...]
    tops = []
    for p in range(2):
        qh, ql = _split_bf16(q[:, p * dkh:(p + 1) * dkh])
        sh, sl = _split_bf16(sub_ref[p])
        nt = lambda a, b: lax.dot_general(a, b, NT_DIMS, preferred_element_type=F32)
        s = nt(sh, qh) + nt(sh, ql) + nt(sl, qh)
        tops.append(_topk_rows(s, topk))
    (s1, i1), (s2, i2) = tops
    width = [topk // (i + 1) for i in range(topk)]
    cand = jnp.concatenate([s1[i:i + 1] + s2[:width[i]] for i in range(topk)], axis=0)
    cidx = jnp.concatenate([i1[i:i + 1] * nk + i2[:width[i]] for i in range(topk)], axis=0)
    rows = lax.broadcasted_iota(jnp.int32, cand.shape, 0)
    top, sel = _topk_rows(cand, topk)
    e_ref[...] = jnp.concatenate(
        [jnp.max(jnp.where(rows == sel[i:i + 1], cidx, -1), axis=0, keepdims=True)
         for i in range(topk)], axis=0)
    ex = jnp.exp(top - jnp.max(top, axis=0, keepdims=True))
    g_ref[...] = ex / jnp.sum(ex, axis=0, keepdims=True)


def peer_route(q, subkeys, layer):
    m, _ = q.shape
    _, heads, _, nk, dkh = subkeys.shape
    tt = _tile(m, 256)
    out = pl.BlockSpec((PEER_TOPK, tt), lambda i, h: (h, i))
    return pl.pallas_call(
        functools.partial(_peer_route_kernel, nk=nk, topk=PEER_TOPK),
        grid=(m // tt, heads),
        in_specs=[pl.BlockSpec((tt, 2 * dkh), lambda i, h: (i, h)),
                  pl.BlockSpec((None, None, 2, nk, dkh), lambda i, h: (layer, h, 0, 0, 0))],
        out_specs=[out, out],
        out_shape=[jax.ShapeDtypeStruct((heads * PEER_TOPK, m), jnp.int32),
                   jax.ShapeDtypeStruct((heads * PEER_TOPK, m), F32)],
        compiler_params=_params("parallel", "parallel"), name="peer_route")(q, subkeys)


def _peer_w_kernel(e_ref, g_ref, w_ref, *, nk):
    tb, hk = e_ref.shape
    key = lax.broadcasted_iota(jnp.int32, (nk, hk), 0)

    def body(t, _):
        e = e_ref[pl.ds(t, 1), :]
        g = g_ref[pl.ds(t, 1), :]
        a = jnp.where(key == e // nk, g, 0.0).astype(BF16)
        b = jnp.where(key == e % nk, 1.0, 0.0).astype(BF16)
        w = lax.dot_general(a, b, NT_DIMS, preferred_element_type=F32)
        w_ref[:, pl.ds(t, 1)] = w.reshape(nk // SUBLANES, 1, SUBLANES, nk)
        return 0

    lax.fori_loop(0, tb, body, 0, unroll=16)


def peer_gate_matrix(eidx, gate, nk):
    m, hk = eidx.shape
    tb = _tile(m, 128)
    row = pl.BlockSpec((tb, hk), lambda i: (i, 0))
    w = pl.pallas_call(
        functools.partial(_peer_w_kernel, nk=nk),
        grid=(m // tb,), in_specs=[row, row],
        out_specs=pl.BlockSpec((nk // SUBLANES, tb, SUBLANES, nk), lambda i: (0, i, 0, 0)),
        out_shape=jax.ShapeDtypeStruct((nk // SUBLANES, m, SUBLANES, nk), F32),
        compiler_params=_params("parallel"), name="peer_w")(eidx, gate)
    return w.reshape(nk // SUBLANES, m * SUBLANES, nk)


def _peer_dense_kernel(h_ref, u_ref, v_ref, w_ref, o_ref, *, tn):
    j = pl.program_id(1)
    tm = h_ref.shape[0]
    nk = w_ref.shape[1]
    act = _gelu(lax.dot_general(h_ref[...], u_ref[...], NT_DIMS, preferred_element_type=F32))
    p = jnp.concatenate(
        [w_ref[pl.ds(a, tm, stride=SUBLANES), :] * act[:, a * nk:(a + 1) * nk]
         for a in range(SUBLANES)], axis=1).astype(BF16)

    @pl.when(j == 0)
    def _():
        o_ref[...] = jnp.zeros_like(o_ref)

    for c in range(0, o_ref.shape[1], tn):
        o_ref[:, c:c + tn] += jnp.dot(p, v_ref[:, c:c + tn], preferred_element_type=F32)


def peer_dense(h, u, v, w, layer):
    m, d = h.shape
    e = u.shape[1]
    nk = w.shape[2]
    tm = _tile(m, 512)
    te = SUBLANES * nk
    assert e == nk * nk and nk % SUBLANES == 0
    once = pl.Buffered(1)
    return pl.pallas_call(
        functools.partial(_peer_dense_kernel, tn=_tile(d, 1024)), grid=(m // tm, e // te),
        in_specs=[pl.BlockSpec((tm, d), lambda i, j: (i, 0), pipeline_mode=once),
                  pl.BlockSpec((None, te, d), lambda i, j: (layer, j, 0)),
                  pl.BlockSpec((None, te, d), lambda i, j: (layer, j, 0)),
                  pl.BlockSpec((None, tm * SUBLANES, nk), lambda i, j: (j, i, 0))],
        out_specs=pl.BlockSpec((tm, d), lambda i, j: (i, 0), pipeline_mode=once),
        out_shape=jax.ShapeDtypeStruct((m, d), F32),
        compiler_params=_params("parallel", "arbitrary"), name="peer_dense")(h, u, v, w)


def peer(h, w_q, subkeys, u, v, layer):
    m = h.shape[0]
    nk = subkeys.shape[3]
    q = matmul(h, w_q, layer)
    mp = -(-m // LANES) * LANES
    e_t, g_t = peer_route(jnp.pad(q, ((0, mp - m), (0, 0))), subkeys, layer)
    w = peer_gate_matrix(e_t.T[:m], g_t.T[:m], nk)
    return peer_dense(h, u, v, w, layer)


def kernel(x_prompt, x_sample, cache_k, cache_v, page_table, norm_mix, norm_ffn, norm_final,
           cm_w_in, cm_norm, cm_w_s, cm_b_s, cm_w_out, sb_w_qkv, sb_bias, sb_w_o,
           peer_w_q, peer_subkeys, peer_u, peer_v):
    b, t, d = x_prompt.shape
    bd, ts, _ = x_sample.shape
    depth = norm_mix.shape[0]
    heads = cache_k.shape[3]
    hd = heads * cache_k.shape[4]
    xp = x_prompt.reshape(b * t, d)
    xs = x_sample.reshape(bd * ts, d)
    assert t % CHUNK == 0 and ts <= CHUNK
    to_bf = lambda a: a.astype(BF16)
    cm_w_in, cm_w_out, sb_w_qkv, sb_w_o, peer_w_q, peer_u, peer_v = map(
        to_bf, (cm_w_in, cm_w_out, sb_w_qkv, sb_w_o, peer_w_q, peer_u, peer_v))

    gate_p, gate_s, kp_rows, vp_rows, ks_rows, vs_rows = [], [], [], [], [], []
    dp = ds = None
    for layer in range(depth):
        li = layer // 2
        g_mix = norm_mix[layer]
        if layer == 0:
            hp, hs = rmsnorm(xp, g_mix), rmsnorm(xs, g_mix)
        else:
            xp, hp = rmsnorm(xp, g_mix, add=dp, emit_sum=True)
            xs, hs = rmsnorm(xs, g_mix, add=ds, emit_sum=True)
        if layer % 2 == 0:
            dg = cm_w_out.shape[1]
            zp = matmul(hp, cm_w_in, li, gelu=True)
            zs = matmul(hs, cm_w_in, li, gelu=True)
            pp, vnp = gating(zp, cm_norm[li], cm_w_s[li], cm_b_s[li])
            zs = jnp.pad(zs.reshape(bd, ts, 2 * dg), ((0, 0), (0, CHUNK - ts), (0, 0)))
            ps, vns = gating(zs.reshape(bd * CHUNK, 2 * dg), cm_norm[li], cm_w_s[li], cm_b_s[li])
            ps = ps.reshape(bd, CHUNK, dg)[:, :ts].reshape(bd * ts, dg)
            gate_p.append(vnp.reshape(b, t, dg)[:, t - CHUNK:])
            gate_s.append(vns.reshape(bd, CHUNK, dg)[:, :ts])
            xp = matmul(pp, cm_w_out, li, res=xp)
            xs = matmul(ps, cm_w_out, li, res=xs)
        else:
            qp = matmul(hp, sb_w_qkv, li, n=hd, col0=0, out_dtypes=(BF16,))
            kp, kpb = matmul(hp, sb_w_qkv, li, n=hd, col0=hd, out_dtypes=(F32, BF16))
            vp, vpb = matmul(hp, sb_w_qkv, li, n=hd, col0=2 * hd, out_dtypes=(F32, BF16))
            op = sb_attention_prompt(qp, kpb, vpb, sb_bias[li], batch=b, heads=heads)
            xp = matmul(op, sb_w_o, li, res=xp)
            qkv_s = matmul(hs, sb_w_qkv, li).reshape(bd, ts, 3, hd)
            qs, kq, vq = qkv_s[:, :, 0], qkv_s[:, :, 1], qkv_s[:, :, 2]
            os_ = sb_attention_sample(qs, kq, vq, sb_bias[li], cache_k, cache_v, page_table,
                                      li, heads=heads)
            xs = matmul(os_.reshape(bd * ts, hd).astype(BF16), sb_w_o, li, res=xs)
            shape5 = lambda a, n0, n1: a.reshape(n0, n1, heads, hd // heads)
            kp_rows.append(shape5(kp, b, t))
            vp_rows.append(shape5(vp, b, t))
            ks_rows.append(shape5(kq, bd, ts))
            vs_rows.append(shape5(vq, bd, ts))
        g_ffn = norm_ffn[layer]
        dp = peer(rmsnorm(xp, g_ffn), peer_w_q, peer_subkeys, peer_u, peer_v, layer)
        ds = peer(rmsnorm(xs, g_ffn), peer_w_q, peer_subkeys, peer_u, peer_v, layer)
    y_prompt = rmsnorm(xp, norm_final, add=dp, out_dtype=F32).reshape(b, t, d)
    y_sample = rmsnorm(xs, norm_final, add=ds, out_dtype=F32).reshape(bd, ts, d)
    return (y_prompt, y_sample, jnp.stack(gate_p), jnp.stack(gate_s),
            jnp.stack(kp_rows), jnp.stack(vp_rows), jnp.stack(ks_rows), jnp.stack(vs_rows))
```

```python
import functools

import jax
import jax.numpy as jnp
import numpy as np
from jax import lax
from jax.experimental import pallas as pl
from jax.experimental.pallas import tpu as pltpu

EPS = 1e-6
CHUNK = 128
PEER_TOPK = 16
V7X_VMEM_LIMIT_BYTES = 56 * 1024 * 1024
LANES = 128
SUBLANES = 8
F32 = jnp.float32
BF16 = jnp.bfloat16
SQRT_HALF = float(np.sqrt(0.5))
NT_DIMS = (((1,), (1,)), ((), ()))


def _params(*sem):
    return pltpu.CompilerParams(dimension_semantics=sem,
                                vmem_limit_bytes=V7X_VMEM_LIMIT_BYTES)


def _gelu(x):
    return 0.5 * x * (1.0 + lax.erf(x * SQRT_HALF))


def _split_bf16(x):
    hi = x.astype(BF16)
    lo = (x - hi.astype(F32)).astype(BF16)
    return hi, lo


def _tile(n, pref):
    return pref if n % pref == 0 else n


def _rmsnorm_kernel(*refs, has_add, emit_sum):
    if has_add:
        x_ref, d_ref, g_ref, *outs = refs
        x = x_ref[...] + d_ref[...]
    else:
        x_ref, g_ref, *outs = refs
        x = x_ref[...]
    if emit_sum:
        outs[0][...] = x
    o_ref = outs[-1]
    ms = jnp.mean(x * x, axis=-1, keepdims=True)
    o_ref[...] = (x * lax.rsqrt(ms + EPS) * g_ref[...]).astype(o_ref.dtype)


def rmsnorm(x, g, *, add=None, emit_sum=False, out_dtype=BF16):
    m, d = x.shape
    tr = _tile(m, 256)
    row = pl.BlockSpec((tr, d), lambda i: (i, 0))
    ins = [x] + ([add] if add is not None else []) + [g.reshape(1, d)]
    in_specs = [row] * (len(ins) - 1) + [pl.BlockSpec((1, d), lambda i: (0, 0))]
    out_shape = [jax.ShapeDtypeStruct((m, d), out_dtype)]
    if emit_sum:
        out_shape = [jax.ShapeDtypeStruct((m, d), F32)] + out_shape
    res = pl.pallas_call(
        functools.partial(_rmsnorm_kernel, has_add=add is not None, emit_sum=emit_sum),
        grid=(m // tr,), in_specs=in_specs, out_specs=[row] * len(out_shape),
        out_shape=out_shape, compiler_params=_params("parallel"), name="rmsnorm")(*ins)
    return res if emit_sum else res[0]


def _matmul_kernel(*refs, has_res, gelu):
    a_ref, w_ref = refs[:2]
    outs = refs[3:] if has_res else refs[2:]
    acc = jnp.dot(a_ref[...], w_ref[...].astype(BF16), preferred_element_type=F32)
    if has_res:
        acc = refs[2][...] + acc
    if gelu:
        acc = _gelu(acc)
    for o in outs:
        o[...] = acc.astype(o.dtype)


def matmul(a, w, layer, *, n=None, col0=0, res=None, gelu=False, out_dtypes=(F32,)):
    m, k = a.shape
    n = w.shape[2] if n is None else n
    tm = _tile(m, 1024)
    tn = _tile(n, 512)
    assert col0 % tn == 0
    cb = col0 // tn
    ins = [a, w] + ([res] if res is not None else [])
    tile = pl.BlockSpec((tm, tn), lambda i, j: (i, j))
    in_specs = [pl.BlockSpec((tm, k), lambda i, j: (i, 0)),
                pl.BlockSpec((None, k, tn), lambda i, j: (layer, 0, j + cb))]
    if res is not None:
        in_specs.append(tile)
    outs = pl.pallas_call(
        functools.partial(_matmul_kernel, has_res=res is not None, gelu=gelu),
        grid=(m // tm, n // tn), in_specs=in_specs,
        out_specs=[tile] * len(out_dtypes),
        out_shape=[jax.ShapeDtypeStruct((m, n), dt) for dt in out_dtypes],
        compiler_params=_params("parallel", "parallel"), name="matmul")(*ins)
    return outs if len(out_dtypes) > 1 else outs[0]


def _gating_kernel(u_ref, v_ref, gn_ref, ws_ref, b_ref, p_ref, vn_ref, *, groups, gd):
    v = v_ref[...]
    ms = jnp.mean(v * v, axis=-1, keepdims=True)
    vn = v * lax.rsqrt(ms + EPS) * gn_ref[...]
    vn_ref[...] = vn
    t_i = lax.broadcasted_iota(jnp.int32, (CHUNK, CHUNK), 0)
    s_i = lax.broadcasted_iota(jnp.int32, (CHUNK, CHUNK), 1)
    causal = s_i <= t_i
    for g in range(groups):
        cols = slice(g * gd, (g + 1) * gd)
        wm = jnp.where(causal, ws_ref[g], 0.0).astype(BF16)
        s = jnp.dot(wm, vn[:, cols].astype(BF16), preferred_element_type=F32)
        s = s + b_ref[:, cols]
        p_ref[:, cols] = (u_ref[:, cols] * s).astype(p_ref.dtype)


def gating(z, g_norm, w_s, b_s):
    rows, dg2 = z.shape
    dg = dg2 // 2
    groups = w_s.shape[0]
    gd = dg // groups
    b_full = jnp.repeat(b_s.T, gd, axis=1)
    half = lambda c: pl.BlockSpec((CHUNK, dg), lambda i: (i, c))
    return pl.pallas_call(
        functools.partial(_gating_kernel, groups=groups, gd=gd),
        grid=(rows // CHUNK,),
        in_specs=[half(0), half(1),
                  pl.BlockSpec((1, dg), lambda i: (0, 0)),
                  pl.BlockSpec((groups, CHUNK, CHUNK), lambda i: (0, 0, 0)),
                  pl.BlockSpec((CHUNK, dg), lambda i: (0, 0))],
        out_specs=[half(0), half(0)],
        out_shape=[jax.ShapeDtypeStruct((rows, dg), BF16),
                   jax.ShapeDtypeStruct((rows, dg), F32)],
        compiler_params=_params("parallel"), name="gating",
    )(z, z, g_norm.reshape(1, dg), w_s, b_full)


def _sb_log_weights(z, tri, mask):
    lg = jnp.log(1.0 + jnp.exp(-jnp.abs(z)))
    l1m = jnp.minimum(-z, 0.0) - lg
    ls = z + l1m
    if mask is not None:
        l1m = jnp.where(mask, l1m, 0.0)
    hi, lo = _split_bf16(l1m)
    excl = (jnp.dot(hi, tri, preferred_element_type=F32)
            + jnp.dot(lo, tri, preferred_element_type=F32))
    pre = ls + excl
    if mask is not None:
        pre = jnp.where(mask, pre, -jnp.inf)
    return pre, jnp.sum(l1m, axis=1, keepdims=True)


def _sb_prompt_kernel(bias_ref, q_ref, k_ref, v_ref, o_ref, *, t, dh, hps, scale):
    h0 = pl.program_id(1) * hps
    qi = pl.program_id(2)
    r_i = lax.broadcasted_iota(jnp.int32, (t, t), 0)
    c_i = lax.broadcasted_iota(jnp.int32, (t, t), 1)
    tri = jnp.where(r_i > c_i, 1.0, 0.0).astype(BF16)
    cols = [slice(x * dh, (x + 1) * dh) for x in range(hps)]
    qs = [(q_ref[:, cs].astype(F32) * scale).astype(BF16) for cs in cols]
    bias = [bias_ref[h0 + x] for x in range(hps)]

    def log_weights(kb, mask):
        ks = pl.multiple_of(kb * t, t)
        out = []
        for x in range(hps):
            z = lax.dot_general(qs[x], k_ref[pl.ds(ks, t), cols[x]], NT_DIMS,
                                preferred_element_type=F32) + bias[x]
            out.append(_sb_log_weights(z, tri, mask))
        return tuple(out)

    def accumulate(kb, logw, carry):
        ks = pl.multiple_of(kb * t, t)
        out = []
        for x, ((pre, rowsum), (c, acc)) in enumerate(zip(logw, carry)):
            a = jnp.exp(pre + c).astype(BF16)
            acc = acc + jnp.dot(a, v_ref[pl.ds(ks, t), cols[x]], preferred_element_type=F32)
            out.append((c + rowsum, acc))
        return tuple(out)

    def body(i, state):
        logw, carry = state
        return log_weights(qi - 1 - i, None), accumulate(qi - i, logw, carry)

    zero = (jnp.zeros((t, 1), F32), jnp.zeros((t, dh), F32))
    logw, carry = lax.fori_loop(0, qi, body, (log_weights(qi, c_i < r_i), (zero,) * hps))
    carry = accumulate(0, logw, carry)
    for x, (_, acc) in enumerate(carry):
        o_ref[:, cols[x]] = acc.astype(o_ref.dtype)


def sb_attention_prompt(q, k, v, bias, *, batch, heads):
    m, hd = q.shape
    t = m // batch
    dh = hd // heads
    tq = _tile(t, 256)
    nq = t // tq
    hps = next(n for n in (4, 2, 1) if heads % n == 0)
    qspec = pl.BlockSpec((tq, hps * dh), lambda b, h, i: (b * nq + i, h))
    kvspec = pl.BlockSpec((t, hps * dh), lambda b, h, i: (b, h))
    return pl.pallas_call(
        functools.partial(_sb_prompt_kernel, t=tq, dh=dh, hps=hps, scale=dh ** -0.5),
        grid=(batch, heads // hps, nq),
        in_specs=[pl.BlockSpec(memory_space=pltpu.SMEM), qspec, kvspec, kvspec],
        out_specs=qspec,
        out_shape=jax.ShapeDtypeStruct((m, hd), BF16),
        compiler_params=_params("parallel", "parallel", "parallel"), name="sb_prompt",
    )(bias.astype(F32), q, k, v)


def _sb_sample_kernel(pt_ref, bias_ref, q_ref, e8_ref, e8t_ref, knew_ref, vnew_ref, *rest,
                      tq, page, pps, scale):
    del pt_ref
    kp_refs, vp_refs = rest[:pps], rest[pps:2 * pps]
    o_ref, c_ref, acc_ref = rest[2 * pps:]
    j = pl.program_id(1)
    ng, dh = knew_ref.shape[1], knew_ref.shape[3]
    gr = SUBLANES * tq
    lanes = page * SUBLANES
    j_i = lax.broadcasted_iota(jnp.int32, (page, page), 0)
    s_i = lax.broadcasted_iota(jnp.int32, (page, page), 1)
    tri = jnp.where(j_i > s_i, 1.0, 0.0).astype(BF16)
    own = (lax.broadcasted_iota(jnp.int32, (gr, lanes), 1) % SUBLANES
           == lax.broadcasted_iota(jnp.int32, (gr, lanes), 0) // tq)

    def blocks(kv_refs, mask, c, acc):
        nb, nr = len(kv_refs), ng * gr
        ss = []
        for k_ref, _ in kv_refs:
            for g in range(ng):
                kg = k_ref[:, g].reshape(lanes, dh).astype(BF16)
                s = lax.dot_general(q_ref[g * gr:(g + 1) * gr, :], kg, NT_DIMS,
                                    preferred_element_type=F32)
                ss.append(jnp.where(own, s, 0.0))
        hi_lo = jnp.concatenate(_split_bf16(jnp.concatenate(ss, axis=0)), axis=0)
        z2 = jnp.dot(hi_lo, e8t_ref[...], preferred_element_type=F32)
        z = ((z2[:nb * nr] + z2[nb * nr:]) * scale
             + jnp.concatenate([bias_ref[...]] * nb, axis=0))
        pre, rowsum = _sb_log_weights(z, tri, mask)
        a = []
        for b in range(nb):
            rows = slice(b * nr, (b + 1) * nr)
            a.append(jnp.exp(pre[rows] + c).astype(BF16))
            c = c + rowsum[rows]
        spread = jnp.dot(jnp.concatenate(a, axis=0), e8_ref[...], preferred_element_type=F32)
        for b, (_, v_ref) in enumerate(kv_refs):
            outs = []
            for g in range(ng):
                r0 = b * nr + g * gr
                ae = jnp.where(own, spread[r0:r0 + gr], 0.0).astype(BF16)
                vg = v_ref[:, g].reshape(lanes, dh).astype(BF16)
                outs.append(jnp.dot(ae, vg, preferred_element_type=F32))
            acc = acc + jnp.concatenate(outs, axis=0)
        c_ref[...] = c
        acc_ref[...] = acc

    @pl.when(j == 0)
    def _():
        key = lax.broadcasted_iota(jnp.int32, (ng * gr, page), 1)
        qry = lax.broadcasted_iota(jnp.int32, (ng * gr, page), 0) % tq
        blocks([(knew_ref, vnew_ref)], key < qry, jnp.zeros(c_ref.shape, F32),
               jnp.zeros(acc_ref.shape, F32))

    @pl.when(j > 0)
    def _():
        blocks([(kp_refs[p], vp_refs[p]) for p in reversed(range(pps))], None,
               c_ref[...], acc_ref[...])

    @pl.when(j == pl.num_programs(1) - 1)
    def _():
        o_ref[...] = acc_ref[...]


def sb_attention_sample(q, k_new, v_new, bias, cache_k, cache_v, page_table, li, *, heads):
    bd, tq, hd = q.shape
    dh = hd // heads
    page = cache_k.shape[2]
    n_pages = page_table.shape[1]
    assert heads % SUBLANES == 0 and tq <= page
    ng = heads // SUBLANES
    nr = heads * tq
    pps = next(p for p in (4, 2, 1) if n_pages % p == 0)
    n_groups = n_pages // pps
    q_rows = q.reshape(bd, tq, heads, dh).transpose(0, 2, 1, 3).reshape(bd, nr, dh).astype(BF16)
    bias_rows = jnp.broadcast_to(jnp.repeat(bias.astype(F32), tq)[:, None], (nr, page))
    e8 = np.kron(np.eye(page, dtype=np.float32), np.ones((1, SUBLANES), np.float32))
    e8, e8t = jnp.asarray(e8, BF16), jnp.asarray(e8.T, BF16)
    grouped = lambda c: c.reshape(*c.shape[:-2], ng, SUBLANES, dh)
    new = lambda x: grouped(jnp.pad(x.reshape(bd, tq, heads, dh),
                                    ((0, 0), (0, page - tq), (0, 0), (0, 0))))

    def page_spec(p):
        def index_map(b, j, pt):
            return (li, pt[b, (n_groups - jnp.maximum(j, 1)) * pps + p], 0, 0, 0, 0)
        return pl.BlockSpec((None, None, page, ng, SUBLANES, dh), index_map)

    const = lambda shape: pl.BlockSpec(shape, lambda b, j, pt: (0,) * len(shape))
    per_b = lambda shape: pl.BlockSpec((None,) + shape, lambda b, j, pt: (b,) + (0,) * len(shape))
    grid_spec = pltpu.PrefetchScalarGridSpec(
        num_scalar_prefetch=1, grid=(bd, n_groups + 1),
        in_specs=[const((nr, page)), per_b((nr, dh)),
                  const((page, page * SUBLANES)), const((page * SUBLANES, page)),
                  per_b((page, ng, SUBLANES, dh)), per_b((page, ng, SUBLANES, dh))]
                 + [page_spec(p) for p in range(pps)] * 2,
        out_specs=per_b((nr, dh)),
        scratch_shapes=[pltpu.VMEM((nr, 1), F32), pltpu.VMEM((nr, dh), F32)])
    o = pl.pallas_call(
        functools.partial(_sb_sample_kernel, tq=tq, page=page, pps=pps, scale=dh ** -0.5),
        grid_spec=grid_spec, out_shape=jax.ShapeDtypeStruct((bd, nr, dh), F32),
        compiler_params=_params("parallel", "arbitrary"), name="sb_sample",
    )(page_table, bias_rows, q_rows, e8, e8t, new(k_new), new(v_new),
      *([grouped(cache_k)] * pps), *([grouped(cache_v)] * pps))
    return o.reshape(bd, heads, tq, dh).transpose(0, 2, 1, 3).reshape(bd, tq, hd)


def _topk_rows(s, k):
    r = s.shape[0]
    rows = lax.broadcasted_iota(jnp.int32, s.shape, 0)
    vals, ids = [], []
    for _ in range(k):
        m = jnp.max(s, axis=0, keepdims=True)
        idx = jnp.min(jnp.where(s == m, rows, r), axis=0, keepdims=True)
        vals.append(m)
        ids.append(idx)
        s = jnp.where(rows == idx, -jnp.inf, s)
    return jnp.concatenate(vals, axis=0), jnp.concatenate(ids, axis=0)


def _peer_route_kernel(q_ref, sub_ref, e_ref, g_ref, *, nk, topk):
    dkh = sub_ref.shape[-1]
    q = q_ref[...]
    tops = []
    for p in range(2):
        qh, ql = _split_bf16(q[:, p * dkh:(p + 1) * dkh])
        sh, sl = _split_bf16(sub_ref[p])
        nt = lambda a, b: lax.dot_general(a, b, NT_DIMS, preferred_element_type=F32)
        s = nt(sh, qh) + nt(sh, ql) + nt(sl, qh)
        tops.append(_topk_rows(s, topk))
    (s1, i1), (s2, i2) = tops
    width = [topk // (i + 1) for i in range(topk)]
    cand = jnp.concatenate([s1[i:i + 1] + s2[:width[i]] for i in range(topk)], axis=0)
    cidx = jnp.concatenate([i1[i:i + 1] * nk + i2[:width[i]] for i in range(topk)], axis=0)
    rows = lax.broadcasted_iota(jnp.int32, cand.shape, 0)
    top, sel = _topk_rows(cand, topk)
    e_ref[...] = jnp.concatenate(
        [jnp.max(jnp.where(rows == sel[i:i + 1], cidx, -1), axis=0, keepdims=True)
         for i in range(topk)], axis=0)
    ex = jnp.exp(top - jnp.max(top, axis=0, keepdims=True))
    g_ref[...] = ex / jnp.sum(ex, axis=0, keepdims=True)


def peer_route(q, subkeys, layer):
    m, _ = q.shape
    _, heads, _, nk, dkh = subkeys.shape
    tt = _tile(m, 256)
    out = pl.BlockSpec((PEER_TOPK, tt), lambda i, h: (h, i))
    return pl.pallas_call(
        functools.partial(_peer_route_kernel, nk=nk, topk=PEER_TOPK),
        grid=(m // tt, heads),
        in_specs=[pl.BlockSpec((tt, 2 * dkh), lambda i, h: (i, h)),
                  pl.BlockSpec((None, None, 2, nk, dkh), lambda i, h: (layer, h, 0, 0, 0))],
        out_specs=[out, out],
        out_shape=[jax.ShapeDtypeStruct((heads * PEER_TOPK, m), jnp.int32),
                   jax.ShapeDtypeStruct((heads * PEER_TOPK, m), F32)],
        compiler_params=_params("parallel", "parallel"), name="peer_route")(q, subkeys)


def _peer_w_kernel(e_ref, g_ref, w_ref, *, nk):
    tb, hk = e_ref.shape
    key = lax.broadcasted_iota(jnp.int32, (nk, hk), 0)

    def body(t, _):
        e = e_ref[pl.ds(t, 1), :]
        g = g_ref[pl.ds(t, 1), :]
        a = jnp.where(key == e // nk, g, 0.0).astype(BF16)
        b = jnp.where(key == e % nk, 1.0, 0.0).astype(BF16)
        w = lax.dot_general(a, b, NT_DIMS, preferred_element_type=F32)
        w_ref[:, pl.ds(t, 1)] = w.reshape(nk // SUBLANES, 1, SUBLANES, nk)
        return 0

    lax.fori_loop(0, tb, body, 0, unroll=16)


def peer_gate_matrix(eidx, gate, nk):
    m, hk = eidx.shape
    tb = _tile(m, 128)
    row = pl.BlockSpec((tb, hk), lambda i: (i, 0))
    w = pl.pallas_call(
        functools.partial(_peer_w_kernel, nk=nk),
        grid=(m // tb,), in_specs=[row, row],
        out_specs=pl.BlockSpec((nk // SUBLANES, tb, SUBLANES, nk), lambda i: (0, i, 0, 0)),
        out_shape=jax.ShapeDtypeStruct((nk // SUBLANES, m, SUBLANES, nk), F32),
        compiler_params=_params("parallel"), name="peer_w")(eidx, gate)
    return w.reshape(nk // SUBLANES, m * SUBLANES, nk)


def _peer_dense_kernel(h_ref, u_ref, v_ref, w_ref, o_ref, *, tn):
    j = pl.program_id(1)
    tm = h_ref.shape[0]
    nk = w_ref.shape[1]
    act = _gelu(lax.dot_general(h_ref[...], u_ref[...], NT_DIMS, preferred_element_type=F32))
    p = jnp.concatenate(
        [w_ref[pl.ds(a, tm, stride=SUBLANES), :] * act[:, a * nk:(a + 1) * nk]
         for a in range(SUBLANES)], axis=1).astype(BF16)

    @pl.when(j == 0)
    def _():
        o_ref[...] = jnp.zeros_like(o_ref)

    for c in range(0, o_ref.shape[1], tn):
        o_ref[:, c:c + tn] += jnp.dot(p, v_ref[:, c:c + tn], preferred_element_type=F32)


def peer_dense(h, u, v, w, layer):
    m, d = h.shape
    e = u.shape[1]
    nk = w.shape[2]
    tm = _tile(m, 512)
    te = SUBLANES * nk
    assert e == nk * nk and nk % SUBLANES == 0
    once = pl.Buffered(1)
    return pl.pallas_call(
        functools.partial(_peer_dense_kernel, tn=_tile(d, 1024)), grid=(m // tm, e // te),
        in_specs=[pl.BlockSpec((tm, d), lambda i, j: (i, 0), pipeline_mode=once),
                  pl.BlockSpec((None, te, d), lambda i, j: (layer, j, 0)),
                  pl.BlockSpec((None, te, d), lambda i, j: (layer, j, 0)),
                  pl.BlockSpec((None, tm * SUBLANES, nk), lambda i, j: (j, i, 0))],
        out_specs=pl.BlockSpec((tm, d), lambda i, j: (i, 0), pipeline_mode=once),
        out_shape=jax.ShapeDtypeStruct((m, d), F32),
        compiler_params=_params("parallel", "arbitrary"), name="peer_dense")(h, u, v, w)


def peer(h, w_q, subkeys, u, v, layer):
    m = h.shape[0]
    nk = subkeys.shape[3]
    q = matmul(h, w_q, layer)
    mp = -(-m // LANES) * LANES
    e_t, g_t = peer_route(jnp.pad(q, ((0, mp - m), (0, 0))), subkeys, layer)
    w = peer_gate_matrix(e_t.T[:m], g_t.T[:m], nk)
    return peer_dense(h, u, v, w, layer)


def kernel(x_prompt, x_sample, cache_k, cache_v, page_table, norm_mix, norm_ffn, norm_final,
           cm_w_in, cm_norm, cm_w_s, cm_b_s, cm_w_out, sb_w_qkv, sb_bias, sb_w_o,
           peer_w_q, peer_subkeys, peer_u, peer_v):
    b, t, d = x_prompt.shape
    bd, ts, _ = x_sample.shape
    depth = norm_mix.shape[0]
    heads = cache_k.shape[3]
    hd = heads * cache_k.shape[4]
    xp = x_prompt.reshape(b * t, d)
    xs = x_sample.reshape(bd * ts, d)
    assert t % CHUNK == 0 and ts <= CHUNK
    peer_u, peer_v = peer_u.astype(BF16), peer_v.astype(BF16)

    gate_p, gate_s, kp_rows, vp_rows, ks_rows, vs_rows = [], [], [], [], [], []
    dp = ds = None
    for layer in range(depth):
        li = layer // 2
        g_mix = norm_mix[layer]
        if layer == 0:
            hp, hs = rmsnorm(xp, g_mix), rmsnorm(xs, g_mix)
        else:
            xp, hp = rmsnorm(xp, g_mix, add=dp, emit_sum=True)
            xs, hs = rmsnorm(xs, g_mix, add=ds, emit_sum=True)
        if layer % 2 == 0:
            dg = cm_w_out.shape[1]
            zp = matmul(hp, cm_w_in, li, gelu=True)
            zs = matmul(hs, cm_w_in, li, gelu=True)
            pp, vnp = gating(zp, cm_norm[li], cm_w_s[li], cm_b_s[li])
            zs = jnp.pad(zs.reshape(bd, ts, 2 * dg), ((0, 0), (0, CHUNK - ts), (0, 0)))
            ps, vns = gating(zs.reshape(bd * CHUNK, 2 * dg), cm_norm[li], cm_w_s[li], cm_b_s[li])
            ps = ps.reshape(bd, CHUNK, dg)[:, :ts].reshape(bd * ts, dg)
            gate_p.append(vnp.reshape(b, t, dg)[:, t - CHUNK:])
            gate_s.append(vns.reshape(bd, CHUNK, dg)[:, :ts])
            xp = matmul(pp, cm_w_out, li, res=xp)
            xs = matmul(ps, cm_w_out, li, res=xs)
        else:
            qp = matmul(hp, sb_w_qkv, li, n=hd, col0=0, out_dtypes=(BF16,))
            kp, kpb = matmul(hp, sb_w_qkv, li, n=hd, col0=hd, out_dtypes=(F32, BF16))
            vp, vpb = matmul(hp, sb_w_qkv, li, n=hd, col0=2 * hd, out_dtypes=(F32, BF16))
            op = sb_attention_prompt(qp, kpb, vpb, sb_bias[li], batch=b, heads=heads)
            xp = matmul(op, sb_w_o, li, res=xp)
            qkv_s = matmul(hs, sb_w_qkv, li).reshape(bd, ts, 3, hd)
            qs, kq, vq = qkv_s[:, :, 0], qkv_s[:, :, 1], qkv_s[:, :, 2]
            os_ = sb_attention_sample(qs, kq, vq, sb_bias[li], cache_k, cache_v, page_table,
                                      li, heads=heads)
            xs = matmul(os_.reshape(bd * ts, hd).astype(BF16), sb_w_o, li, res=xs)
            shape5 = lambda a, n0, n1: a.reshape(n0, n1, heads, hd // heads)
            kp_rows.append(shape5(kp, b, t))
            vp_rows.append(shape5(vp, b, t))
            ks_rows.append(shape5(kq, bd, ts))
            vs_rows.append(shape5(vq, bd, ts))
        g_ffn = norm_ffn[layer]
        dp = peer(rmsnorm(xp, g_ffn), peer_w_q, peer_subkeys, peer_u, peer_v, layer)
        ds = peer(rmsnorm(xs, g_ffn), peer_w_q, peer_subkeys, peer_u, peer_v, layer)
    y_prompt = rmsnorm(xp, norm_final, add=dp, out_dtype=F32).reshape(b, t, d)
    y_sample = rmsnorm(xs, norm_final, add=ds, out_dtype=F32).reshape(bd, ts, d)
    return (y_prompt, y_sample, jnp.stack(gate_p), jnp.stack(gate_s),
            jnp.stack(kp_rows), jnp.stack(vp_rows), jnp.stack(ks_rows), jnp.stack(vs_rows))
```

```python
import functools

import jax
import jax.numpy as jnp
import numpy as np
from jax import lax
from jax.experimental import pallas as pl
from jax.experimental.pallas import tpu as pltpu

EPS = 1e-6
CHUNK = 128
PEER_TOPK = 16
V7X_VMEM_LIMIT_BYTES = 56 * 1024 * 1024
LANES = 128
SUBLANES = 8
F32 = jnp.float32
BF16 = jnp.bfloat16
SQRT_HALF = float(np.sqrt(0.5))
NT_DIMS = (((1,), (1,)), ((), ()))


def _params(*sem):
    return pltpu.CompilerParams(dimension_semantics=sem,
                                vmem_limit_bytes=V7X_VMEM_LIMIT_BYTES)


def _gelu(x):
    return 0.5 * x * (1.0 + lax.erf(x * SQRT_HALF))


def _split_bf16(x):
    hi = x.astype(BF16)
    lo = (x - hi.astype(F32)).astype(BF16)
    return hi, lo


def _tile(n, pref):
    return pref if n % pref == 0 else n


def _rmsnorm_kernel(x_ref, g_ref, o_ref):
    x = x_ref[...]
    ms = jnp.mean(x * x, axis=-1, keepdims=True)
    o_ref[...] = (x * lax.rsqrt(ms + EPS) * g_ref[...]).astype(o_ref.dtype)


def rmsnorm(x, g, *, out_dtype=BF16):
    m, d = x.shape
    tr = _tile(m, 256)
    row = pl.BlockSpec((tr, d), lambda i: (i, 0))
    return pl.pallas_call(
        _rmsnorm_kernel, grid=(m // tr,),
        in_specs=[row, pl.BlockSpec((1, d), lambda i: (0, 0))], out_specs=row,
        out_shape=jax.ShapeDtypeStruct((m, d), out_dtype),
        compiler_params=_params("parallel"), name="rmsnorm")(x, g.reshape(1, d))


def _matmul_kernel(*refs, has_res, gelu):
    a_ref, w_ref = refs[:2]
    outs = refs[3:] if has_res else refs[2:]
    acc = jnp.dot(a_ref[...], w_ref[...], preferred_element_type=F32)
    if has_res:
        acc = refs[2][...] + acc
    if gelu:
        acc = _gelu(acc)
    for o in outs:
        o[...] = acc.astype(o.dtype)


def matmul(a, w, layer, *, n=None, col0=0, res=None, gelu=False, out_dtypes=(F32,)):
    m, k = a.shape
    n = w.shape[2] if n is None else n
    tm = _tile(m, 1024)
    tn = _tile(n, 512 if (res is not None or len(out_dtypes) > 1) else 1024)
    assert col0 % tn == 0
    cb = col0 // tn
    ins = [a, w] + ([res] if res is not None else [])
    tile = pl.BlockSpec((tm, tn), lambda i, j: (i, j))
    in_specs = [pl.BlockSpec((tm, k), lambda i, j: (i, 0)),
                pl.BlockSpec((None, k, tn), lambda i, j: (layer, 0, j + cb))]
    if res is not None:
        in_specs.append(tile)
    outs = pl.pallas_call(
        functools.partial(_matmul_kernel, has_res=res is not None, gelu=gelu),
        grid=(m // tm, n // tn), in_specs=in_specs,
        out_specs=[tile] * len(out_dtypes),
        out_shape=[jax.ShapeDtypeStruct((m, n), dt) for dt in out_dtypes],
        compiler_params=_params("parallel", "parallel"), name="matmul")(*ins)
    return outs if len(out_dtypes) > 1 else outs[0]


def _gating_kernel(u_ref, v_ref, gn_ref, ws_ref, b_ref, p_ref, vn_ref, *, groups, gd):
    v = v_ref[...]
    ms = jnp.mean(v * v, axis=-1, keepdims=True)
    vn = v * lax.rsqrt(ms + EPS) * gn_ref[...]
    vn_ref[...] = vn
    t_i = lax.broadcasted_iota(jnp.int32, (CHUNK, CHUNK), 0)
    s_i = lax.broadcasted_iota(jnp.int32, (CHUNK, CHUNK), 1)
    causal = s_i <= t_i
    for g in range(groups):
        cols = slice(g * gd, (g + 1) * gd)
        wm = jnp.where(causal, ws_ref[g], 0.0).astype(BF16)
        s = jnp.dot(wm, vn[:, cols].astype(BF16), preferred_element_type=F32)
        s = s + b_ref[:, cols]
        p_ref[:, cols] = (u_ref[:, cols] * s).astype(p_ref.dtype)


def gating(z, g_norm, w_s, b_s):
    rows, dg2 = z.shape
    dg = dg2 // 2
    groups = w_s.shape[0]
    gd = dg // groups
    b_full = jnp.repeat(b_s.T, gd, axis=1)
    half = lambda c: pl.BlockSpec((CHUNK, dg), lambda i: (i, c))
    return pl.pallas_call(
        functools.partial(_gating_kernel, groups=groups, gd=gd),
        grid=(rows // CHUNK,),
        in_specs=[half(0), half(1),
                  pl.BlockSpec((1, dg), lambda i: (0, 0)),
                  pl.BlockSpec((groups, CHUNK, CHUNK), lambda i: (0, 0, 0)),
                  pl.BlockSpec((CHUNK, dg), lambda i: (0, 0))],
        out_specs=[half(0), half(0)],
        out_shape=[jax.ShapeDtypeStruct((rows, dg), BF16),
                   jax.ShapeDtypeStruct((rows, dg), F32)],
        compiler_params=_params("parallel"), name="gating",
    )(z, z, g_norm.reshape(1, dg), w_s, b_full)


def _sb_log_weights(z, tri, mask):
    lg = jnp.log(1.0 + jnp.exp(-jnp.abs(z)))
    l1m = jnp.minimum(-z, 0.0) - lg
    ls = z + l1m
    if mask is not None:
        l1m = jnp.where(mask, l1m, 0.0)
    hi, lo = _split_bf16(l1m)
    excl = (jnp.dot(hi, tri, preferred_element_type=F32)
            + jnp.dot(lo, tri, preferred_element_type=F32))
    pre = ls + excl
    if mask is not None:
        pre = jnp.where(mask, pre, -jnp.inf)
    return pre, jnp.sum(l1m, axis=1, keepdims=True)


def _sb_prompt_kernel(bias_ref, q_ref, k_ref, v_ref, o_ref, *, t, dh, hps, scale):
    h0 = pl.program_id(1) * hps
    qi = pl.program_id(2)
    r_i = lax.broadcasted_iota(jnp.int32, (t, t), 0)
    c_i = lax.broadcasted_iota(jnp.int32, (t, t), 1)
    tri = jnp.where(r_i > c_i, 1.0, 0.0).astype(BF16)
    cols = [slice(x * dh, (x + 1) * dh) for x in range(hps)]
    qs = [(q_ref[:, cs].astype(F32) * scale).astype(BF16) for cs in cols]
    bias = [bias_ref[h0 + x] for x in range(hps)]

    def log_weights(kb, mask):
        ks = pl.multiple_of(kb * t, t)
        out = []
        for x in range(hps):
            z = lax.dot_general(qs[x], k_ref[pl.ds(ks, t), cols[x]], NT_DIMS,
                                preferred_element_type=F32) + bias[x]
            out.append(_sb_log_weights(z, tri, mask))
        return tuple(out)

    def accumulate(kb, logw, carry):
        ks = pl.multiple_of(kb * t, t)
        out = []
        for x, ((pre, rowsum), (c, acc)) in enumerate(zip(logw, carry)):
            a = jnp.exp(pre + c).astype(BF16)
            acc = acc + jnp.dot(a, v_ref[pl.ds(ks, t), cols[x]], preferred_element_type=F32)
            out.append((c + rowsum, acc))
        return tuple(out)

    def body(i, state):
        logw, carry = state
        return log_weights(qi - 1 - i, None), accumulate(qi - i, logw, carry)

    zero = (jnp.zeros((t, 1), F32), jnp.zeros((t, dh), F32))
    logw, carry = lax.fori_loop(0, qi, body, (log_weights(qi, c_i < r_i), (zero,) * hps))
    carry = accumulate(0, logw, carry)
    for x, (_, acc) in enumerate(carry):
        o_ref[:, cols[x]] = acc.astype(o_ref.dtype)


def sb_attention_prompt(q, k, v, bias, *, batch, heads):
    m, hd = q.shape
    t = m // batch
    dh = hd // heads
    tq = _tile(t, 256)
    nq = t // tq
    hps = next(n for n in (4, 2, 1) if heads % n == 0)
    qspec = pl.BlockSpec((tq, hps * dh), lambda b, h, i: (b * nq + i, h))
    kvspec = pl.BlockSpec((t, hps * dh), lambda b, h, i: (b, h))
    return pl.pallas_call(
        functools.partial(_sb_prompt_kernel, t=tq, dh=dh, hps=hps, scale=dh ** -0.5),
        grid=(batch, heads // hps, nq),
        in_specs=[pl.BlockSpec(memory_space=pltpu.SMEM), qspec, kvspec, kvspec],
        out_specs=qspec,
        out_shape=jax.ShapeDtypeStruct((m, hd), BF16),
        compiler_params=_params("parallel", "parallel", "parallel"), name="sb_prompt",
    )(bias.astype(F32), q, k, v)


def _sb_sample_kernel(pt_ref, bias_ref, q_ref, e8_ref, e8t_ref, knew_ref, vnew_ref, *rest,
                      tq, page, pps, scale):
    del pt_ref
    kp_refs, vp_refs = rest[:pps], rest[pps:2 * pps]
    o_ref, c_ref, acc_ref = rest[2 * pps:]
    j = pl.program_id(1)
    ng, dh = knew_ref.shape[1], knew_ref.shape[3]
    gr = SUBLANES * tq
    lanes = page * SUBLANES
    j_i = lax.broadcasted_iota(jnp.int32, (page, page), 0)
    s_i = lax.broadcasted_iota(jnp.int32, (page, page), 1)
    tri = jnp.where(j_i > s_i, 1.0, 0.0).astype(BF16)
    own = (lax.broadcasted_iota(jnp.int32, (gr, lanes), 1) % SUBLANES
           == lax.broadcasted_iota(jnp.int32, (gr, lanes), 0) // tq)

    def blocks(kv_refs, mask, c, acc):
        nb, nr = len(kv_refs), ng * gr
        ss = []
        for k_ref, _ in kv_refs:
            for g in range(ng):
                kg = k_ref[:, g].reshape(lanes, dh).astype(BF16)
                s = lax.dot_general(q_ref[g * gr:(g + 1) * gr, :], kg, NT_DIMS,
                                    preferred_element_type=F32)
                ss.append(jnp.where(own, s, 0.0))
        hi_lo = jnp.concatenate(_split_bf16(jnp.concatenate(ss, axis=0)), axis=0)
        z2 = jnp.dot(hi_lo, e8t_ref[...], preferred_element_type=F32)
        z = ((z2[:nb * nr] + z2[nb * nr:]) * scale
             + jnp.concatenate([bias_ref[...]] * nb, axis=0))
        pre, rowsum = _sb_log_weights(z, tri, mask)
        a = []
        for b in range(nb):
            rows = slice(b * nr, (b + 1) * nr)
            a.append(jnp.exp(pre[rows] + c).astype(BF16))
            c = c + rowsum[rows]
        spread = jnp.dot(jnp.concatenate(a, axis=0), e8_ref[...], preferred_element_type=F32)
        for b, (_, v_ref) in enumerate(kv_refs):
            outs = []
            for g in range(ng):
                r0 = b * nr + g * gr
                ae = jnp.where(own, spread[r0:r0 + gr], 0.0).astype(BF16)
                vg = v_ref[:, g].reshape(lanes, dh).astype(BF16)
                outs.append(jnp.dot(ae, vg, preferred_element_type=F32))
            acc = acc + jnp.concatenate(outs, axis=0)
        c_ref[...] = c
        acc_ref[...] = acc

    @pl.when(j == 0)
    def _():
        key = lax.broadcasted_iota(jnp.int32, (ng * gr, page), 1)
        qry = lax.broadcasted_iota(jnp.int32, (ng * gr, page), 0) % tq
        blocks([(knew_ref, vnew_ref)], key < qry, jnp.zeros(c_ref.shape, F32),
               jnp.zeros(acc_ref.shape, F32))

    @pl.when(j > 0)
    def _():
        blocks([(kp_refs[p], vp_refs[p]) for p in reversed(range(pps))], None,
               c_ref[...], acc_ref[...])

    @pl.when(j == pl.num_programs(1) - 1)
    def _():
        o_ref[...] = acc_ref[...]


def sb_attention_sample(q, k_new, v_new, bias, cache_k, cache_v, page_table, li, *, heads):
    bd, tq, hd = q.shape
    dh = hd // heads
    page = cache_k.shape[2]
    n_pages = page_table.shape[1]
    assert heads % SUBLANES == 0 and tq <= page
    ng = heads // SUBLANES
    nr = heads * tq
    pps = next(p for p in (4, 2, 1) if n_pages % p == 0)
    n_groups = n_pages // pps
    q_rows = q.reshape(bd, tq, heads, dh).transpose(0, 2, 1, 3).reshape(bd, nr, dh).astype(BF16)
    bias_rows = jnp.broadcast_to(jnp.repeat(bias.astype(F32), tq)[:, None], (nr, page))
    e8 = np.kron(np.eye(page, dtype=np.float32), np.ones((1, SUBLANES), np.float32))
    e8, e8t = jnp.asarray(e8, BF16), jnp.asarray(e8.T, BF16)
    grouped = lambda c: c.reshape(*c.shape[:-2], ng, SUBLANES, dh)
    new = lambda x: grouped(jnp.pad(x.reshape(bd, tq, heads, dh),
                                    ((0, 0), (0, page - tq), (0, 0), (0, 0))))

    def page_spec(p):
        def index_map(b, j, pt):
            return (li, pt[b, (n_groups - jnp.maximum(j, 1)) * pps + p], 0, 0, 0, 0)
        return pl.BlockSpec((None, None, page, ng, SUBLANES, dh), index_map)

    const = lambda shape: pl.BlockSpec(shape, lambda b, j, pt: (0,) * len(shape))
    per_b = lambda shape: pl.BlockSpec((None,) + shape, lambda b, j, pt: (b,) + (0,) * len(shape))
    grid_spec = pltpu.PrefetchScalarGridSpec(
        num_scalar_prefetch=1, grid=(bd, n_groups + 1),
        in_specs=[const((nr, page)), per_b((nr, dh)),
                  const((page, page * SUBLANES)), const((page * SUBLANES, page)),
                  per_b((page, ng, SUBLANES, dh)), per_b((page, ng, SUBLANES, dh))]
                 + [page_spec(p) for p in range(pps)] * 2,
        out_specs=per_b((nr, dh)),
        scratch_shapes=[pltpu.VMEM((nr, 1), F32), pltpu.VMEM((nr, dh), F32)])
    o = pl.pallas_call(
        functools.partial(_sb_sample_kernel, tq=tq, page=page, pps=pps, scale=dh ** -0.5),
        grid_spec=grid_spec, out_shape=jax.ShapeDtypeStruct((bd, nr, dh), F32),
        compiler_params=_params("parallel", "arbitrary"), name="sb_sample",
    )(page_table, bias_rows, q_rows, e8, e8t, new(k_new), new(v_new),
      *([grouped(cache_k)] * pps), *([grouped(cache_v)] * pps))
    return o.reshape(bd, heads, tq, dh).transpose(0, 2, 1, 3).reshape(bd, tq, hd)


def _topk_rows(s, k):
    r = s.shape[0]
    rows = lax.broadcasted_iota(jnp.int32, s.shape, 0)
    vals, ids = [], []
    for _ in range(k):
        m = jnp.max(s, axis=0, keepdims=True)
        idx = jnp.min(jnp.where(s == m, rows, r), axis=0, keepdims=True)
        vals.append(m)
        ids.append(idx)
        s = jnp.where(rows == idx, -jnp.inf, s)
    return jnp.concatenate(vals, axis=0), jnp.concatenate(ids, axis=0)


def _peer_route_kernel(q_ref, sub_ref, e_ref, g_ref, *, nk, topk):
    dkh = sub_ref.shape[-1]
    q = q_ref[...]
    tops = []
    for p in range(2):
        qh, ql = _split_bf16(q[:, p * dkh:(p + 1) * dkh])
        sh, sl = _split_bf16(sub_ref[p])
        nt = lambda a, b: lax.dot_general(a, b, NT_DIMS, preferred_element_type=F32)
        s = nt(sh, qh) + nt(sh, ql) + nt(sl, qh)
        tops.append(_topk_rows(s, topk))
    (s1, i1), (s2, i2) = tops
    width = [topk // (i + 1) for i in range(topk)]
    cand = jnp.concatenate([s1[i:i + 1] + s2[:width[i]] for i in range(topk)], axis=0)
    cidx = jnp.concatenate([i1[i:i + 1] * nk + i2[:width[i]] for i in range(topk)], axis=0)
    rows = lax.broadcasted_iota(jnp.int32, cand.shape, 0)
    top, sel = _topk_rows(cand, topk)
    e_ref[...] = jnp.concatenate(
        [jnp.max(jnp.where(rows == sel[i:i + 1], cidx, -1), axis=0, keepdims=True)
         for i in range(topk)], axis=0)
    ex = jnp.exp(top - jnp.max(top, axis=0, keepdims=True))
    g_ref[...] = ex / jnp.sum(ex, axis=0, keepdims=True)


def peer_route(q, subkeys, layer):
    m, _ = q.shape
    _, heads, _, nk, dkh = subkeys.shape
    tt = _tile(m, 256)
    out = pl.BlockSpec((PEER_TOPK, tt), lambda i, h: (h, i))
    return pl.pallas_call(
        functools.partial(_peer_route_kernel, nk=nk, topk=PEER_TOPK),
        grid=(m // tt, heads),
        in_specs=[pl.BlockSpec((tt, 2 * dkh), lambda i, h: (i, h)),
                  pl.BlockSpec((None, None, 2, nk, dkh), lambda i, h: (layer, h, 0, 0, 0))],
        out_specs=[out, out],
        out_shape=[jax.ShapeDtypeStruct((heads * PEER_TOPK, m), jnp.int32),
                   jax.ShapeDtypeStruct((heads * PEER_TOPK, m), F32)],
        compiler_params=_params("parallel", "parallel"), name="peer_route")(q, subkeys)


def _peer_w_kernel(e_ref, g_ref, w_ref, *, nk):
    tb, hk = e_ref.shape
    key = lax.broadcasted_iota(jnp.int32, (nk, hk), 0)

    def body(t, _):
        e = e_ref[pl.ds(t, 1), :]
        g = g_ref[pl.ds(t, 1), :]
        a = jnp.where(key == e // nk, g, 0.0).astype(BF16)
        b = jnp.where(key == e % nk, 1.0, 0.0).astype(BF16)
        w = lax.dot_general(a, b, NT_DIMS, preferred_element_type=F32)
        w_ref[:, pl.ds(t, 1)] = w.reshape(nk // SUBLANES, 1, SUBLANES, nk)
        return 0

    lax.fori_loop(0, tb, body, 0, unroll=16)


def peer_gate_matrix(eidx, gate, nk):
    m, hk = eidx.shape
    tb = _tile(m, 128)
    row = pl.BlockSpec((tb, hk), lambda i: (i, 0))
    w = pl.pallas_call(
        functools.partial(_peer_w_kernel, nk=nk),
        grid=(m // tb,), in_specs=[row, row],
        out_specs=pl.BlockSpec((nk // SUBLANES, tb, SUBLANES, nk), lambda i: (0, i, 0, 0)),
        out_shape=jax.ShapeDtypeStruct((nk // SUBLANES, m, SUBLANES, nk), F32),
        compiler_params=_params("parallel"), name="peer_w")(eidx, gate)
    return w.reshape(nk // SUBLANES, m * SUBLANES, nk)


def _peer_scores_kernel(h_ref, u_ref, w_ref, p_ref):
    tm = h_ref.shape[0]
    nk = w_ref.shape[1]
    act = _gelu(lax.dot_general(h_ref[...], u_ref[...], NT_DIMS, preferred_element_type=F32))
    for a in range(SUBLANES):
        cols = slice(a * nk, (a + 1) * nk)
        p_ref[:, cols] = (w_ref[pl.ds(a, tm, stride=SUBLANES), :] * act[:, cols]).astype(p_ref.dtype)


def peer_scores(h, u, w, layer):
    m, d = h.shape
    e = u.shape[1]
    nk = w.shape[2]
    tm = _tile(m, 1024)
    te = SUBLANES * nk
    assert e == nk * nk and nk % SUBLANES == 0
    return pl.pallas_call(
        _peer_scores_kernel, grid=(m // tm, e // te),
        in_specs=[pl.BlockSpec((tm, d), lambda i, j: (i, 0)),
                  pl.BlockSpec((None, te, d), lambda i, j: (layer, j, 0)),
                  pl.BlockSpec((None, tm * SUBLANES, nk), lambda i, j: (j, i, 0))],
        out_specs=pl.BlockSpec((tm, te), lambda i, j: (i, j)),
        out_shape=jax.ShapeDtypeStruct((m, e), BF16),
        compiler_params=_params("parallel", "parallel"), name="peer_scores")(h, u, w)


def _peer_combine_kernel(x_ref, p_ref, v_ref, o_ref):
    @pl.when(pl.program_id(2) == 0)
    def _():
        o_ref[...] = x_ref[...]

    o_ref[...] += jnp.dot(p_ref[...], v_ref[...], preferred_element_type=F32)


def peer_combine(x, p, v, layer):
    m, d = x.shape
    e = p.shape[1]
    tm, tn, tk = _tile(m, 1024), _tile(d, 1024), _tile(e, 4096)
    tile = pl.BlockSpec((tm, tn), lambda i, j, k: (i, j))
    return pl.pallas_call(
        _peer_combine_kernel, grid=(m // tm, d // tn, e // tk),
        in_specs=[tile, pl.BlockSpec((tm, tk), lambda i, j, k: (i, k)),
                  pl.BlockSpec((None, tk, tn), lambda i, j, k: (layer, k, j))],
        out_specs=tile, out_shape=jax.ShapeDtypeStruct((m, d), F32),
        compiler_params=_params("parallel", "parallel", "arbitrary"), name="peer_combine")(x, p, v)


def peer(x, g, w_q, subkeys, u, v, layer):
    m = x.shape[0]
    nk = subkeys.shape[3]
    h = rmsnorm(x, g)
    q = matmul(h, w_q, layer)
    mp = -(-m // LANES) * LANES
    e_t, g_t = peer_route(jnp.pad(q, ((0, mp - m), (0, 0))), subkeys, layer)
    w = peer_gate_matrix(e_t.T[:m], g_t.T[:m], nk)
    return peer_combine(x, peer_scores(h, u, w, layer), v, layer)


def kernel(x_prompt, x_sample, cache_k, cache_v, page_table, norm_mix, norm_ffn, norm_final,
           cm_w_in, cm_norm, cm_w_s, cm_b_s, cm_w_out, sb_w_qkv, sb_bias, sb_w_o,
           peer_w_q, peer_subkeys, peer_u, peer_v):
    b, t, d = x_prompt.shape
    bd, ts, _ = x_sample.shape
    depth = norm_mix.shape[0]
    heads = cache_k.shape[3]
    hd = heads * cache_k.shape[4]
    xp = x_prompt.reshape(b * t, d)
    xs = x_sample.reshape(bd * ts, d)
    assert t % CHUNK == 0 and ts <= CHUNK
    to_bf = lambda a: a.astype(BF16)
    cm_w_in, cm_w_out, sb_w_qkv, sb_w_o, peer_w_q, peer_u, peer_v = map(
        to_bf, (cm_w_in, cm_w_out, sb_w_qkv, sb_w_o, peer_w_q, peer_u, peer_v))

    gate_p, gate_s, kp_rows, vp_rows, ks_rows, vs_rows = [], [], [], [], [], []
    for layer in range(depth):
        li = layer // 2
        hp, hs = rmsnorm(xp, norm_mix[layer]), rmsnorm(xs, norm_mix[layer])
        if layer % 2 == 0:
            dg = cm_w_out.shape[1]
            zp = matmul(hp, cm_w_in, li, gelu=True)
            zs = matmul(hs, cm_w_in, li, gelu=True)
            pp, vnp = gating(zp, cm_norm[li], cm_w_s[li], cm_b_s[li])
            zs = jnp.pad(zs.reshape(bd, ts, 2 * dg), ((0, 0), (0, CHUNK - ts), (0, 0)))
            ps, vns = gating(zs.reshape(bd * CHUNK, 2 * dg), cm_norm[li], cm_w_s[li], cm_b_s[li])
            ps = ps.reshape(bd, CHUNK, dg)[:, :ts].reshape(bd * ts, dg)
            gate_p.append(vnp.reshape(b, t, dg)[:, t - CHUNK:])
            gate_s.append(vns.reshape(bd, CHUNK, dg)[:, :ts])
            xp = matmul(pp, cm_w_out, li, res=xp)
            xs = matmul(ps, cm_w_out, li, res=xs)
        else:
            qp = matmul(hp, sb_w_qkv, li, n=hd, col0=0, out_dtypes=(BF16,))
            kp, kpb = matmul(hp, sb_w_qkv, li, n=hd, col0=hd, out_dtypes=(F32, BF16))
            vp, vpb = matmul(hp, sb_w_qkv, li, n=hd, col0=2 * hd, out_dtypes=(F32, BF16))
            op = sb_attention_prompt(qp, kpb, vpb, sb_bias[li], batch=b, heads=heads)
            xp = matmul(op, sb_w_o, li, res=xp)
            qkv_s = matmul(hs, sb_w_qkv, li).reshape(bd, ts, 3, hd)
            qs, kq, vq = qkv_s[:, :, 0], qkv_s[:, :, 1], qkv_s[:, :, 2]
            os_ = sb_attention_sample(qs, kq, vq, sb_bias[li], cache_k, cache_v, page_table,
                                      li, heads=heads)
            xs = matmul(os_.reshape(bd * ts, hd).astype(BF16), sb_w_o, li, res=xs)
            shape5 = lambda a, n0, n1: a.reshape(n0, n1, heads, hd // heads)
            kp_rows.append(shape5(kp, b, t))
            vp_rows.append(shape5(vp, b, t))
            ks_rows.append(shape5(kq, bd, ts))
            vs_rows.append(shape5(vq, bd, ts))
        xp = peer(xp, norm_ffn[layer], peer_w_q, peer_subkeys, peer_u, peer_v, layer)
        xs = peer(xs, norm_ffn[layer], peer_w_q, peer_subkeys, peer_u, peer_v, layer)
    y_prompt = rmsnorm(xp, norm_final, out_dtype=F32).reshape(b, t, d)
    y_sample = rmsnorm(xs, norm_final, out_dtype=F32).reshape(bd, ts, d)
    return (y_prompt, y_sample, jnp.stack(gate_p), jnp.stack(gate_s),
            jnp.stack(kp_rows), jnp.stack(vp_rows), jnp.stack(ks_rows), jnp.stack(vs_rows))
```

```python
import functools

import jax
import jax.numpy as jnp
import numpy as np
from jax import lax
from jax.experimental import pallas as pl
from jax.experimental.pallas import tpu as pltpu

EPS = 1e-6
CHUNK = 128
PEER_TOPK = 16
V7X_VMEM_LIMIT_BYTES = 56 * 1024 * 1024
LANES = 128
SUBLANES = 8
F32 = jnp.float32
BF16 = jnp.bfloat16
SQRT_HALF = float(np.sqrt(0.5))
NT_DIMS = (((1,), (1,)), ((), ()))


def _params(*sem):
    return pltpu.CompilerParams(dimension_semantics=sem,
                                vmem_limit_bytes=V7X_VMEM_LIMIT_BYTES)


def _gelu(x):
    return 0.5 * x * (1.0 + lax.erf(x * SQRT_HALF))


def _split_bf16(x):
    hi = x.astype(BF16)
    lo = (x - hi.astype(F32)).astype(BF16)
    return hi, lo


def _tile(n, pref):
    return pref if n % pref == 0 else n


def _rmsnorm_kernel(x_ref, g_ref, o_ref):
    x = x_ref[...]
    ms = jnp.mean(x * x, axis=-1, keepdims=True)
    o_ref[...] = (x * lax.rsqrt(ms + EPS) * g_ref[...]).astype(o_ref.dtype)


def rmsnorm(x, g, *, out_dtype=BF16):
    m, d = x.shape
    tr = _tile(m, 256)
    row = pl.BlockSpec((tr, d), lambda i: (i, 0))
    return pl.pallas_call(
        _rmsnorm_kernel, grid=(m // tr,),
        in_specs=[row, pl.BlockSpec((1, d), lambda i: (0, 0))], out_specs=row,
        out_shape=jax.ShapeDtypeStruct((m, d), out_dtype),
        compiler_params=_params("parallel"), name="rmsnorm")(x, g.reshape(1, d))


def _matmul_kernel(*refs, has_res, gelu):
    a_ref, w_ref = refs[:2]
    outs = refs[3:] if has_res else refs[2:]
    acc = jnp.dot(a_ref[...], w_ref[...], preferred_element_type=F32)
    if has_res:
        acc = refs[2][...] + acc
    if gelu:
        acc = _gelu(acc)
    for o in outs:
        o[...] = acc.astype(o.dtype)


def matmul(a, w, layer, *, n=None, col0=0, res=None, gelu=False, out_dtypes=(F32,)):
    m, k = a.shape
    n = w.shape[2] if n is None else n
    tm = _tile(m, 1024)
    tn = _tile(n, 512 if (res is not None or len(out_dtypes) > 1) else 1024)
    assert col0 % tn == 0
    cb = col0 // tn
    ins = [a, w] + ([res] if res is not None else [])
    tile = pl.BlockSpec((tm, tn), lambda i, j: (i, j))
    in_specs = [pl.BlockSpec((tm, k), lambda i, j: (i, 0)),
                pl.BlockSpec((None, k, tn), lambda i, j: (layer, 0, j + cb))]
    if res is not None:
        in_specs.append(tile)
    outs = pl.pallas_call(
        functools.partial(_matmul_kernel, has_res=res is not None, gelu=gelu),
        grid=(m // tm, n // tn), in_specs=in_specs,
        out_specs=[tile] * len(out_dtypes),
        out_shape=[jax.ShapeDtypeStruct((m, n), dt) for dt in out_dtypes],
        compiler_params=_params("parallel", "parallel"), name="matmul")(*ins)
    return outs if len(out_dtypes) > 1 else outs[0]


def _gating_kernel(u_ref, v_ref, gn_ref, ws_ref, b_ref, p_ref, vn_ref, *, groups, gd):
    v = v_ref[...]
    ms = jnp.mean(v * v, axis=-1, keepdims=True)
    vn = v * lax.rsqrt(ms + EPS) * gn_ref[...]
    vn_ref[...] = vn
    t_i = lax.broadcasted_iota(jnp.int32, (CHUNK, CHUNK), 0)
    s_i = lax.broadcasted_iota(jnp.int32, (CHUNK, CHUNK), 1)
    causal = s_i <= t_i
    for g in range(groups):
        cols = slice(g * gd, (g + 1) * gd)
        wm = jnp.where(causal, ws_ref[g], 0.0).astype(BF16)
        s = jnp.dot(wm, vn[:, cols].astype(BF16), preferred_element_type=F32)
        s = s + b_ref[:, cols]
        p_ref[:, cols] = (u_ref[:, cols] * s).astype(p_ref.dtype)


def gating(z, g_norm, w_s, b_s):
    rows, dg2 = z.shape
    dg = dg2 // 2
    groups = w_s.shape[0]
    gd = dg // groups
    b_full = jnp.repeat(b_s.T, gd, axis=1)
    half = lambda c: pl.BlockSpec((CHUNK, dg), lambda i: (i, c))
    return pl.pallas_call(
        functools.partial(_gating_kernel, groups=groups, gd=gd),
        grid=(rows // CHUNK,),
        in_specs=[half(0), half(1),
                  pl.BlockSpec((1, dg), lambda i: (0, 0)),
                  pl.BlockSpec((groups, CHUNK, CHUNK), lambda i: (0, 0, 0)),
                  pl.BlockSpec((CHUNK, dg), lambda i: (0, 0))],
        out_specs=[half(0), half(0)],
        out_shape=[jax.ShapeDtypeStruct((rows, dg), BF16),
                   jax.ShapeDtypeStruct((rows, dg), F32)],
        compiler_params=_params("parallel"), name="gating",
    )(z, z, g_norm.reshape(1, dg), w_s, b_full)


def _sb_log_weights(z, tri, mask):
    lg = jnp.log(1.0 + jnp.exp(-jnp.abs(z)))
    l1m = jnp.minimum(-z, 0.0) - lg
    ls = z + l1m
    if mask is not None:
        l1m = jnp.where(mask, l1m, 0.0)
    hi, lo = _split_bf16(l1m)
    excl = (jnp.dot(hi, tri, preferred_element_type=F32)
            + jnp.dot(lo, tri, preferred_element_type=F32))
    pre = ls + excl
    if mask is not None:
        pre = jnp.where(mask, pre, -jnp.inf)
    return pre, jnp.sum(l1m, axis=1, keepdims=True)


def _sb_prompt_kernel(bias_ref, q_ref, k_ref, v_ref, o_ref, *, t, dh, hps, scale):
    h0 = pl.program_id(1) * hps
    nq = q_ref.shape[0] // t
    r_i = lax.broadcasted_iota(jnp.int32, (t, t), 0)
    c_i = lax.broadcasted_iota(jnp.int32, (t, t), 1)
    tri = jnp.where(r_i > c_i, 1.0, 0.0).astype(BF16)
    cols = [slice(x * dh, (x + 1) * dh) for x in range(hps)]
    bias = [bias_ref[h0 + x] for x in range(hps)]

    def log_weights(qi, kb, mask):
        ks = pl.multiple_of(kb * t, t)
        out = []
        for x in range(hps):
            q = (q_ref[qi * t:(qi + 1) * t, cols[x]].astype(F32) * scale).astype(BF16)
            z = lax.dot_general(q, k_ref[pl.ds(ks, t), cols[x]], NT_DIMS,
                                preferred_element_type=F32) + bias[x]
            out.append(_sb_log_weights(z, tri, mask))
        return tuple(out)

    def accumulate(kb, logw, carry):
        ks = pl.multiple_of(kb * t, t)
        out = []
        for x, ((pre, rowsum), (c, acc)) in enumerate(zip(logw, carry)):
            a = jnp.exp(pre + c).astype(BF16)
            acc = acc + jnp.dot(a, v_ref[pl.ds(ks, t), cols[x]], preferred_element_type=F32)
            out.append((c + rowsum, acc))
        return tuple(out)

    zero = (jnp.zeros((t, 1), F32), jnp.zeros((t, dh), F32))
    logw = log_weights(0, 0, c_i < r_i)
    for qi in range(nq):
        def body(i, state, qi=qi):
            logw, carry = state
            return log_weights(qi, qi - 1 - i, None), accumulate(qi - i, logw, carry)

        logw, carry = lax.fori_loop(0, qi, body, (logw, (zero,) * hps))
        nxt = log_weights(qi + 1, qi + 1, c_i < r_i) if qi + 1 < nq else None
        for x, (_, acc) in enumerate(accumulate(0, logw, carry)):
            o_ref[qi * t:(qi + 1) * t, cols[x]] = acc.astype(o_ref.dtype)
        logw = nxt


def sb_attention_prompt(q, k, v, bias, *, batch, heads):
    m, hd = q.shape
    t = m // batch
    dh = hd // heads
    hps = next(n for n in (4, 2, 1) if heads % n == 0)
    seq = pl.BlockSpec((t, hps * dh), lambda b, h: (b, h))
    return pl.pallas_call(
        functools.partial(_sb_prompt_kernel, t=_tile(t, 256), dh=dh, hps=hps, scale=dh ** -0.5),
        grid=(batch, heads // hps),
        in_specs=[pl.BlockSpec(memory_space=pltpu.SMEM), seq, seq, seq],
        out_specs=seq,
        out_shape=jax.ShapeDtypeStruct((m, hd), BF16),
        compiler_params=_params("parallel", "parallel"), name="sb_prompt",
    )(bias.astype(F32), q, k, v)


def _sb_sample_kernel(pt_ref, bias_ref, q_ref, e8_ref, e8t_ref, knew_ref, vnew_ref, *rest,
                      tq, page, pps, scale):
    del pt_ref
    kp_refs, vp_refs = rest[:pps], rest[pps:2 * pps]
    o_ref, c_ref, acc_ref = rest[2 * pps:]
    j = pl.program_id(1)
    ng, dh = knew_ref.shape[1], knew_ref.shape[3]
    gr = SUBLANES * tq
    lanes = page * SUBLANES
    j_i = lax.broadcasted_iota(jnp.int32, (page, page), 0)
    s_i = lax.broadcasted_iota(jnp.int32, (page, page), 1)
    tri = jnp.where(j_i > s_i, 1.0, 0.0).astype(BF16)
    own = (lax.broadcasted_iota(jnp.int32, (gr, lanes), 1) % SUBLANES
           == lax.broadcasted_iota(jnp.int32, (gr, lanes), 0) // tq)

    def blocks(kv_refs, mask, c, acc):
        nb, nr = len(kv_refs), ng * gr
        ss = []
        for k_ref, _ in kv_refs:
            for g in range(ng):
                kg = k_ref[:, g].reshape(lanes, dh).astype(BF16)
                s = lax.dot_general(q_ref[g * gr:(g + 1) * gr, :], kg, NT_DIMS,
                                    preferred_element_type=F32)
                ss.append(jnp.where(own, s, 0.0))
        hi_lo = jnp.concatenate(_split_bf16(jnp.concatenate(ss, axis=0)), axis=0)
        z2 = jnp.dot(hi_lo, e8t_ref[...], preferred_element_type=F32)
        z = ((z2[:nb * nr] + z2[nb * nr:]) * scale
             + jnp.concatenate([bias_ref[...]] * nb, axis=0))
        pre, rowsum = _sb_log_weights(z, tri, mask)
        a = []
        for b in range(nb):
            rows = slice(b * nr, (b + 1) * nr)
            a.append(jnp.exp(pre[rows] + c).astype(BF16))
            c = c + rowsum[rows]
        spread = jnp.dot(jnp.concatenate(a, axis=0), e8_ref[...], preferred_element_type=F32)
        for b, (_, v_ref) in enumerate(kv_refs):
            outs = []
            for g in range(ng):
                r0 = b * nr + g * gr
                ae = jnp.where(own, spread[r0:r0 + gr], 0.0).astype(BF16)
                vg = v_ref[:, g].reshape(lanes, dh).astype(BF16)
                outs.append(jnp.dot(ae, vg, preferred_element_type=F32))
            acc = acc + jnp.concatenate(outs, axis=0)
        c_ref[...] = c
        acc_ref[...] = acc

    @pl.when(j == 0)
    def _():
        key = lax.broadcasted_iota(jnp.int32, (ng * gr, page), 1)
        qry = lax.broadcasted_iota(jnp.int32, (ng * gr, page), 0) % tq
        blocks([(knew_ref, vnew_ref)], key < qry, jnp.zeros(c_ref.shape, F32),
               jnp.zeros(acc_ref.shape, F32))

    @pl.when(j > 0)
    def _():
        blocks([(kp_refs[p], vp_refs[p]) for p in reversed(range(pps))], None,
               c_ref[...], acc_ref[...])

    @pl.when(j == pl.num_programs(1) - 1)
    def _():
        o_ref[...] = acc_ref[...]


def sb_attention_sample(q, k_new, v_new, bias, cache_k, cache_v, page_table, li, *, heads):
    bd, tq, hd = q.shape
    dh = hd // heads
    page = cache_k.shape[2]
    n_pages = page_table.shape[1]
    assert heads % SUBLANES == 0 and tq <= page
    ng = heads // SUBLANES
    nr = heads * tq
    pps = next(p for p in (4, 2, 1) if n_pages % p == 0)
    n_groups = n_pages // pps
    q_rows = q.reshape(bd, tq, heads, dh).transpose(0, 2, 1, 3).reshape(bd, nr, dh).astype(BF16)
    bias_rows = jnp.broadcast_to(jnp.repeat(bias.astype(F32), tq)[:, None], (nr, page))
    e8 = np.kron(np.eye(page, dtype=np.float32), np.ones((1, SUBLANES), np.float32))
    e8, e8t = jnp.asarray(e8, BF16), jnp.asarray(e8.T, BF16)
    grouped = lambda c: c.reshape(*c.shape[:-2], ng, SUBLANES, dh)
    new = lambda x: grouped(jnp.pad(x.reshape(bd, tq, heads, dh),
                                    ((0, 0), (0, page - tq), (0, 0), (0, 0))))

    def page_spec(p):
        def index_map(b, j, pt):
            return (li, pt[b, (n_groups - jnp.maximum(j, 1)) * pps + p], 0, 0, 0, 0)
        return pl.BlockSpec((None, None, page, ng, SUBLANES, dh), index_map)

    const = lambda shape: pl.BlockSpec(shape, lambda b, j, pt: (0,) * len(shape))
    per_b = lambda shape: pl.BlockSpec((None,) + shape, lambda b, j, pt: (b,) + (0,) * len(shape))
    grid_spec = pltpu.PrefetchScalarGridSpec(
        num_scalar_prefetch=1, grid=(bd, n_groups + 1),
        in_specs=[const((nr, page)), per_b((nr, dh)),
                  const((page, page * SUBLANES)), const((page * SUBLANES, page)),
                  per_b((page, ng, SUBLANES, dh)), per_b((page, ng, SUBLANES, dh))]
                 + [page_spec(p) for p in range(pps)] * 2,
        out_specs=per_b((nr, dh)),
        scratch_shapes=[pltpu.VMEM((nr, 1), F32), pltpu.VMEM((nr, dh), F32)])
    o = pl.pallas_call(
        functools.partial(_sb_sample_kernel, tq=tq, page=page, pps=pps, scale=dh ** -0.5),
        grid_spec=grid_spec, out_shape=jax.ShapeDtypeStruct((bd, nr, dh), F32),
        compiler_params=_params("parallel", "arbitrary"), name="sb_sample",
    )(page_table, bias_rows, q_rows, e8, e8t, new(k_new), new(v_new),
      *([grouped(cache_k)] * pps), *([grouped(cache_v)] * pps))
    return o.reshape(bd, heads, tq, dh).transpose(0, 2, 1, 3).reshape(bd, tq, hd)


def _topk_rows(s, k):
    r = s.shape[0]
    rows = lax.broadcasted_iota(jnp.int32, s.shape, 0)
    vals, ids = [], []
    for _ in range(k):
        m = jnp.max(s, axis=0, keepdims=True)
        idx = jnp.min(jnp.where(s == m, rows, r), axis=0, keepdims=True)
        vals.append(m)
        ids.append(idx)
        s = jnp.where(rows == idx, -jnp.inf, s)
    return jnp.concatenate(vals, axis=0), jnp.concatenate(ids, axis=0)


def _peer_route_kernel(q_ref, sub_ref, e_ref, g_ref, *, nk, topk):
    dkh = sub_ref.shape[-1]
    q = q_ref[...]
    tops = []
    for p in range(2):
        qh, ql = _split_bf16(q[:, p * dkh:(p + 1) * dkh])
        sh, sl = _split_bf16(sub_ref[p])
        nt = lambda a, b: lax.dot_general(a, b, NT_DIMS, preferred_element_type=F32)
        s = nt(sh, qh) + nt(sh, ql) + nt(sl, qh)
        tops.append(_topk_rows(s, topk))
    (s1, i1), (s2, i2) = tops
    width = [topk // (i + 1) for i in range(topk)]
    cand = jnp.concatenate([s1[i:i + 1] + s2[:width[i]] for i in range(topk)], axis=0)
    cidx = jnp.concatenate([i1[i:i + 1] * nk + i2[:width[i]] for i in range(topk)], axis=0)
    rows = lax.broadcasted_iota(jnp.int32, cand.shape, 0)
    top, sel = _topk_rows(cand, topk)
    e_ref[...] = jnp.concatenate(
        [jnp.max(jnp.where(rows == sel[i:i + 1], cidx, -1), axis=0, keepdims=True)
         for i in range(topk)], axis=0)
    ex = jnp.exp(top - jnp.max(top, axis=0, keepdims=True))
    g_ref[...] = ex / jnp.sum(ex, axis=0, keepdims=True)


def peer_route(q, subkeys, layer):
    m, _ = q.shape
    _, heads, _, nk, dkh = subkeys.shape
    tt = _tile(m, 256)
    out = pl.BlockSpec((PEER_TOPK, tt), lambda i, h: (h, i))
    return pl.pallas_call(
        functools.partial(_peer_route_kernel, nk=nk, topk=PEER_TOPK),
        grid=(m // tt, heads),
        in_specs=[pl.BlockSpec((tt, 2 * dkh), lambda i, h: (i, h)),
                  pl.BlockSpec((None, None, 2, nk, dkh), lambda i, h: (layer, h, 0, 0, 0))],
        out_specs=[out, out],
        out_shape=[jax.ShapeDtypeStruct((heads * PEER_TOPK, m), jnp.int32),
                   jax.ShapeDtypeStruct((heads * PEER_TOPK, m), F32)],
        compiler_params=_params("parallel", "parallel"), name="peer_route")(q, subkeys)


def _peer_w_kernel(e_ref, g_ref, w_ref, *, nk):
    tb, hk = e_ref.shape
    key = lax.broadcasted_iota(jnp.int32, (nk, hk), 0)

    def body(t, _):
        e = e_ref[pl.ds(t, 1), :]
        g = g_ref[pl.ds(t, 1), :]
        a = jnp.where(key == e // nk, g, 0.0).astype(BF16)
        b = jnp.where(key == e % nk, 1.0, 0.0).astype(BF16)
        w = lax.dot_general(a, b, NT_DIMS, preferred_element_type=F32)
        w_ref[:, pl.ds(t, 1)] = w.reshape(nk // SUBLANES, 1, SUBLANES, nk)
        return 0

    lax.fori_loop(0, tb, body, 0, unroll=16)


def peer_gate_matrix(eidx, gate, nk):
    m, hk = eidx.shape
    tb = _tile(m, 128)
    row = pl.BlockSpec((tb, hk), lambda i: (i, 0))
    w = pl.pallas_call(
        functools.partial(_peer_w_kernel, nk=nk),
        grid=(m // tb,), in_specs=[row, row],
        out_specs=pl.BlockSpec((nk // SUBLANES, tb, SUBLANES, nk), lambda i: (0, i, 0, 0)),
        out_shape=jax.ShapeDtypeStruct((nk // SUBLANES, m, SUBLANES, nk), F32),
        compiler_params=_params("parallel"), name="peer_w")(eidx, gate)
    return w.reshape(nk // SUBLANES, m * SUBLANES, nk)


def _peer_scores_kernel(h_ref, u_ref, w_ref, p_ref):
    tm = h_ref.shape[0]
    nk = w_ref.shape[1]
    act = _gelu(lax.dot_general(h_ref[...], u_ref[...], NT_DIMS, preferred_element_type=F32))
    for a in range(SUBLANES):
        cols = slice(a * nk, (a + 1) * nk)
        p_ref[:, cols] = (w_ref[pl.ds(a, tm, stride=SUBLANES), :] * act[:, cols]).astype(p_ref.dtype)


def peer_scores(h, u, w, layer):
    m, d = h.shape
    e = u.shape[1]
    nk = w.shape[2]
    tm = _tile(m, 1024)
    te = SUBLANES * nk
    assert e == nk * nk and nk % SUBLANES == 0
    return pl.pallas_call(
        _peer_scores_kernel, grid=(m // tm, e // te),
        in_specs=[pl.BlockSpec((tm, d), lambda i, j: (i, 0)),
                  pl.BlockSpec((None, te, d), lambda i, j: (layer, j, 0)),
                  pl.BlockSpec((None, tm * SUBLANES, nk), lambda i, j: (j, i, 0))],
        out_specs=pl.BlockSpec((tm, te), lambda i, j: (i, j)),
        out_shape=jax.ShapeDtypeStruct((m, e), BF16),
        compiler_params=_params("parallel", "parallel"), name="peer_scores")(h, u, w)


def _peer_combine_kernel(x_ref, p_ref, v_ref, o_ref):
    @pl.when(pl.program_id(2) == 0)
    def _():
        o_ref[...] = x_ref[...]

    o_ref[...] += jnp.dot(p_ref[...], v_ref[...], preferred_element_type=F32)


def peer_combine(x, p, v, layer):
    m, d = x.shape
    e = p.shape[1]
    tm, tn, tk = _tile(m, 1024), _tile(d, 1024), _tile(e, 4096)
    tile = pl.BlockSpec((tm, tn), lambda i, j, k: (i, j))
    return pl.pallas_call(
        _peer_combine_kernel, grid=(m // tm, d // tn, e // tk),
        in_specs=[tile, pl.BlockSpec((tm, tk), lambda i, j, k: (i, k)),
                  pl.BlockSpec((None, tk, tn), lambda i, j, k: (layer, k, j))],
        out_specs=tile, out_shape=jax.ShapeDtypeStruct((m, d), F32),
        compiler_params=_params("parallel", "parallel", "arbitrary"), name="peer_combine")(x, p, v)


def peer(x, g, w_q, subkeys, u, v, layer):
    m = x.shape[0]
    nk = subkeys.shape[3]
    h = rmsnorm(x, g)
    q = matmul(h, w_q, layer)
    mp = -(-m // LANES) * LANES
    e_t, g_t = peer_route(jnp.pad(q, ((0, mp - m), (0, 0))), subkeys, layer)
    w = peer_gate_matrix(e_t.T[:m], g_t.T[:m], nk)
    return peer_combine(x, peer_scores(h, u, w, layer), v, layer)


def kernel(x_prompt, x_sample, cache_k, cache_v, page_table, norm_mix, norm_ffn, norm_final,
           cm_w_in, cm_norm, cm_w_s, cm_b_s, cm_w_out, sb_w_qkv, sb_bias, sb_w_o,
           peer_w_q, peer_subkeys, peer_u, peer_v):
    b, t, d = x_prompt.shape
    bd, ts, _ = x_sample.shape
    depth = norm_mix.shape[0]
    heads = cache_k.shape[3]
    hd = heads * cache_k.shape[4]
    xp = x_prompt.reshape(b * t, d)
    xs = x_sample.reshape(bd * ts, d)
    assert t % CHUNK == 0 and ts <= CHUNK
    to_bf = lambda a: a.astype(BF16)
    cm_w_in, cm_w_out, sb_w_qkv, sb_w_o, peer_w_q, peer_u, peer_v = map(
        to_bf, (cm_w_in, cm_w_out, sb_w_qkv, sb_w_o, peer_w_q, peer_u, peer_v))

    gate_p, gate_s, kp_rows, vp_rows, ks_rows, vs_rows = [], [], [], [], [], []
    for layer in range(depth):
        li = layer // 2
        hp, hs = rmsnorm(xp, norm_mix[layer]), rmsnorm(xs, norm_mix[layer])
        if layer % 2 == 0:
            dg = cm_w_out.shape[1]
            zp = matmul(hp, cm_w_in, li, gelu=True)
            zs = matmul(hs, cm_w_in, li, gelu=True)
            pp, vnp = gating(zp, cm_norm[li], cm_w_s[li], cm_b_s[li])
            zs = jnp.pad(zs.reshape(bd, ts, 2 * dg), ((0, 0), (0, CHUNK - ts), (0, 0)))
            ps, vns = gating(zs.reshape(bd * CHUNK, 2 * dg), cm_norm[li], cm_w_s[li], cm_b_s[li])
            ps = ps.reshape(bd, CHUNK, dg)[:, :ts].reshape(bd * ts, dg)
            gate_p.append(vnp.reshape(b, t, dg)[:, t - CHUNK:])
            gate_s.append(vns.reshape(bd, CHUNK, dg)[:, :ts])
            xp = matmul(pp, cm_w_out, li, res=xp)
            xs = matmul(ps, cm_w_out, li, res=xs)
        else:
            qp = matmul(hp, sb_w_qkv, li, n=hd, col0=0, out_dtypes=(BF16,))
            kp, kpb = matmul(hp, sb_w_qkv, li, n=hd, col0=hd, out_dtypes=(F32, BF16))
            vp, vpb = matmul(hp, sb_w_qkv, li, n=hd, col0=2 * hd, out_dtypes=(F32, BF16))
            op = sb_attention_prompt(qp, kpb, vpb, sb_bias[li], batch=b, heads=heads)
            xp = matmul(op, sb_w_o, li, res=xp)
            qkv_s = matmul(hs, sb_w_qkv, li).reshape(bd, ts, 3, hd)
            qs, kq, vq = qkv_s[:, :, 0], qkv_s[:, :, 1], qkv_s[:, :, 2]
            os_ = sb_attention_sample(qs, kq, vq, sb_bias[li], cache_k, cache_v, page_table,
                                      li, heads=heads)
            xs = matmul(os_.reshape(bd * ts, hd).astype(BF16), sb_w_o, li, res=xs)
            shape5 = lambda a, n0, n1: a.reshape(n0, n1, heads, hd // heads)
            kp_rows.append(shape5(kp, b, t))
            vp_rows.append(shape5(vp, b, t))
            ks_rows.append(shape5(kq, bd, ts))
            vs_rows.append(shape5(vq, bd, ts))
        xp = peer(xp, norm_ffn[layer], peer_w_q, peer_subkeys, peer_u, peer_v, layer)
        xs = peer(xs, norm_ffn[layer], peer_w_q, peer_subkeys, peer_u, peer_v, layer)
    y_prompt = rmsnorm(xp, norm_final, out_dtype=F32).reshape(b, t, d)
    y_sample = rmsnorm(xs, norm_final, out_dtype=F32).reshape(bd, ts, d)
    return (y_prompt, y_sample, jnp.stack(gate_p), jnp.stack(gate_s),
            jnp.stack(kp_rows), jnp.stack(vp_rows), jnp.stack(ks_rows), jnp.stack(vs_rows))
```

```python
import functools

import jax
import jax.numpy as jnp
import numpy as np
from jax import lax
from jax.experimental import pallas as pl
from jax.experimental.pallas import tpu as pltpu

EPS = 1e-6
CHUNK = 128
PEER_TOPK = 16
V7X_VMEM_LIMIT_BYTES = 56 * 1024 * 1024
LANES = 128
SUBLANES = 8
F32 = jnp.float32
BF16 = jnp.bfloat16
SQRT_HALF = float(np.sqrt(0.5))
NT_DIMS = (((1,), (1,)), ((), ()))


def _params(*sem):
    return pltpu.CompilerParams(dimension_semantics=sem,
                                vmem_limit_bytes=V7X_VMEM_LIMIT_BYTES)


def _gelu(x):
    return 0.5 * x * (1.0 + lax.erf(x * SQRT_HALF))


def _split_bf16(x):
    hi = x.astype(BF16)
    lo = (x - hi.astype(F32)).astype(BF16)
    return hi, lo


def _tile(n, pref):
    return pref if n % pref == 0 else n


def _rmsnorm_kernel(x_ref, g_ref, o_ref):
    x = x_ref[...]
    ms = jnp.mean(x * x, axis=-1, keepdims=True)
    o_ref[...] = (x * lax.rsqrt(ms + EPS) * g_ref[...]).astype(o_ref.dtype)


def rmsnorm(x, g, *, out_dtype=BF16):
    m, d = x.shape
    tr = _tile(m, 256)
    row = pl.BlockSpec((tr, d), lambda i: (i, 0))
    return pl.pallas_call(
        _rmsnorm_kernel, grid=(m // tr,),
        in_specs=[row, pl.BlockSpec((1, d), lambda i: (0, 0))], out_specs=row,
        out_shape=jax.ShapeDtypeStruct((m, d), out_dtype),
        compiler_params=_params("parallel"), name="rmsnorm")(x, g.reshape(1, d))


def _matmul_kernel(*refs, has_res, gelu):
    a_ref, w_ref = refs[:2]
    outs = refs[3:] if has_res else refs[2:]
    acc = jnp.dot(a_ref[...], w_ref[...], preferred_element_type=F32)
    if has_res:
        acc = refs[2][...] + acc
    if gelu:
        acc = _gelu(acc)
    for o in outs:
        o[...] = acc.astype(o.dtype)


def matmul(a, w, layer, *, n=None, col0=0, res=None, gelu=False, out_dtypes=(F32,)):
    m, k = a.shape
    n = w.shape[2] if n is None else n
    tm = _tile(m, 1024)
    tn = _tile(n, 512 if (res is not None or len(out_dtypes) > 1) else 1024)
    assert col0 % tn == 0
    cb = col0 // tn
    ins = [a, w] + ([res] if res is not None else [])
    tile = pl.BlockSpec((tm, tn), lambda i, j: (i, j))
    in_specs = [pl.BlockSpec((tm, k), lambda i, j: (i, 0)),
                pl.BlockSpec((None, k, tn), lambda i, j: (layer, 0, j + cb))]
    if res is not None:
        in_specs.append(tile)
    outs = pl.pallas_call(
        functools.partial(_matmul_kernel, has_res=res is not None, gelu=gelu),
        grid=(m // tm, n // tn), in_specs=in_specs,
        out_specs=[tile] * len(out_dtypes),
        out_shape=[jax.ShapeDtypeStruct((m, n), dt) for dt in out_dtypes],
        compiler_params=_params("parallel", "parallel"), name="matmul")(*ins)
    return outs if len(out_dtypes) > 1 else outs[0]


def _gating_kernel(u_ref, v_ref, gn_ref, ws_ref, b_ref, p_ref, vn_ref, *, groups, gd):
    v = v_ref[...]
    ms = jnp.mean(v * v, axis=-1, keepdims=True)
    vn = v * lax.rsqrt(ms + EPS) * gn_ref[...]
    vn_ref[...] = vn
    t_i = lax.broadcasted_iota(jnp.int32, (CHUNK, CHUNK), 0)
    s_i = lax.broadcasted_iota(jnp.int32, (CHUNK, CHUNK), 1)
    causal = s_i <= t_i
    for g in range(groups):
        cols = slice(g * gd, (g + 1) * gd)
        wm = jnp.where(causal, ws_ref[g], 0.0).astype(BF16)
        s = jnp.dot(wm, vn[:, cols].astype(BF16), preferred_element_type=F32)
        s = s + b_ref[:, cols]
        p_ref[:, cols] = (u_ref[:, cols] * s).astype(p_ref.dtype)


def gating(z, g_norm, w_s, b_s):
    rows, dg2 = z.shape
    dg = dg2 // 2
    groups = w_s.shape[0]
    gd = dg // groups
    b_full = jnp.repeat(b_s.T, gd, axis=1)
    half = lambda c: pl.BlockSpec((CHUNK, dg), lambda i: (i, c))
    return pl.pallas_call(
        functools.partial(_gating_kernel, groups=groups, gd=gd),
        grid=(rows // CHUNK,),
        in_specs=[half(0), half(1),
                  pl.BlockSpec((1, dg), lambda i: (0, 0)),
                  pl.BlockSpec((groups, CHUNK, CHUNK), lambda i: (0, 0, 0)),
                  pl.BlockSpec((CHUNK, dg), lambda i: (0, 0))],
        out_specs=[half(0), half(0)],
        out_shape=[jax.ShapeDtypeStruct((rows, dg), BF16),
                   jax.ShapeDtypeStruct((rows, dg), F32)],
        compiler_params=_params("parallel"), name="gating",
    )(z, z, g_norm.reshape(1, dg), w_s, b_full)


def _sb_log_weights(z, tri, mask):
    lg = jnp.log(1.0 + jnp.exp(-jnp.abs(z)))
    l1m = jnp.minimum(-z, 0.0) - lg
    ls = z + l1m
    if mask is not None:
        l1m = jnp.where(mask, l1m, 0.0)
    hi, lo = _split_bf16(l1m)
    excl = (jnp.dot(hi, tri, preferred_element_type=F32)
            + jnp.dot(lo, tri, preferred_element_type=F32))
    pre = ls + excl
    if mask is not None:
        pre = jnp.where(mask, pre, -jnp.inf)
    return pre, jnp.sum(l1m, axis=1, keepdims=True)


def _sb_prompt_kernel(bias_ref, q_ref, k_ref, v_ref, o_ref, *, t, dh, hps, scale):
    h0 = pl.program_id(1) * hps
    nq = q_ref.shape[0] // t
    r_i = lax.broadcasted_iota(jnp.int32, (t, t), 0)
    c_i = lax.broadcasted_iota(jnp.int32, (t, t), 1)
    tri = jnp.where(r_i > c_i, 1.0, 0.0).astype(BF16)
    cols = [slice(x * dh, (x + 1) * dh) for x in range(hps)]
    bias = [bias_ref[h0 + x] for x in range(hps)]

    def log_weights(qi, kb, mask):
        ks = pl.multiple_of(kb * t, t)
        out = []
        for x in range(hps):
            q = (q_ref[qi * t:(qi + 1) * t, cols[x]].astype(F32) * scale).astype(BF16)
            z = lax.dot_general(q, k_ref[pl.ds(ks, t), cols[x]], NT_DIMS,
                                preferred_element_type=F32) + bias[x]
            out.append(_sb_log_weights(z, tri, mask))
        return tuple(out)

    def accumulate(kb, logw, carry):
        ks = pl.multiple_of(kb * t, t)
        out = []
        for x, ((pre, rowsum), (c, acc)) in enumerate(zip(logw, carry)):
            a = jnp.exp(pre + c).astype(BF16)
            acc = acc + jnp.dot(a, v_ref[pl.ds(ks, t), cols[x]], preferred_element_type=F32)
            out.append((c + rowsum, acc))
        return tuple(out)

    zero = (jnp.zeros((t, 1), F32), jnp.zeros((t, dh), F32))
    logw = log_weights(0, 0, c_i < r_i)
    for qi in range(nq):
        def body(i, state, qi=qi):
            logw, carry = state
            return log_weights(qi, qi - 1 - i, None), accumulate(qi - i, logw, carry)

        logw, carry = lax.fori_loop(0, qi, body, (logw, (zero,) * hps))
        nxt = log_weights(qi + 1, qi + 1, c_i < r_i) if qi + 1 < nq else None
        for x, (_, acc) in enumerate(accumulate(0, logw, carry)):
            o_ref[qi * t:(qi + 1) * t, cols[x]] = acc.astype(o_ref.dtype)
        logw = nxt


def sb_attention_prompt(q, k, v, bias, *, batch, heads):
    m, hd = q.shape
    t = m // batch
    dh = hd // heads
    hps = next(n for n in (8, 4, 2, 1) if heads % n == 0)
    seq = pl.BlockSpec((t, hps * dh), lambda b, h: (b, h))
    return pl.pallas_call(
        functools.partial(_sb_prompt_kernel, t=_tile(t, 256), dh=dh, hps=hps, scale=dh ** -0.5),
        grid=(batch, heads // hps),
        in_specs=[pl.BlockSpec(memory_space=pltpu.SMEM), seq, seq, seq],
        out_specs=seq,
        out_shape=jax.ShapeDtypeStruct((m, hd), BF16),
        compiler_params=_params("parallel", "parallel"), name="sb_prompt",
    )(bias.astype(F32), q, k, v)


def _sb_sample_kernel(pt_ref, bias_ref, q_ref, e8_ref, e8t_ref, knew_ref, vnew_ref, *rest,
                      tq, page, pps, scale):
    del pt_ref
    kp_refs, vp_refs = rest[:pps], rest[pps:2 * pps]
    o_ref, c_ref, acc_ref = rest[2 * pps:]
    j = pl.program_id(1)
    ng, dh = knew_ref.shape[1], knew_ref.shape[3]
    gr = SUBLANES * tq
    lanes = page * SUBLANES
    j_i = lax.broadcasted_iota(jnp.int32, (page, page), 0)
    s_i = lax.broadcasted_iota(jnp.int32, (page, page), 1)
    tri = jnp.where(j_i > s_i, 1.0, 0.0).astype(BF16)
    own = (lax.broadcasted_iota(jnp.int32, (gr, lanes), 1) % SUBLANES
           == lax.broadcasted_iota(jnp.int32, (gr, lanes), 0) // tq)

    def blocks(kv_refs, mask, c, acc):
        nb, nr = len(kv_refs), ng * gr
        ss = []
        for k_ref, _ in kv_refs:
            for g in range(ng):
                kg = k_ref[:, g].reshape(lanes, dh).astype(BF16)
                s = lax.dot_general(q_ref[g * gr:(g + 1) * gr, :], kg, NT_DIMS,
                                    preferred_element_type=F32)
                ss.append(jnp.where(own, s, 0.0))
        hi_lo = jnp.concatenate(_split_bf16(jnp.concatenate(ss, axis=0)), axis=0)
        z2 = jnp.dot(hi_lo, e8t_ref[...], preferred_element_type=F32)
        z = ((z2[:nb * nr] + z2[nb * nr:]) * scale
             + jnp.concatenate([bias_ref[...]] * nb, axis=0))
        pre, rowsum = _sb_log_weights(z, tri, mask)
        a = []
        for b in range(nb):
            rows = slice(b * nr, (b + 1) * nr)
            a.append(jnp.exp(pre[rows] + c).astype(BF16))
            c = c + rowsum[rows]
        spread = jnp.dot(jnp.concatenate(a, axis=0), e8_ref[...], preferred_element_type=F32)
        for b, (_, v_ref) in enumerate(kv_refs):
            outs = []
            for g in range(ng):
                r0 = b * nr + g * gr
                ae = jnp.where(own, spread[r0:r0 + gr], 0.0).astype(BF16)
                vg = v_ref[:, g].reshape(lanes, dh).astype(BF16)
                outs.append(jnp.dot(ae, vg, preferred_element_type=F32))
            acc = acc + jnp.concatenate(outs, axis=0)
        c_ref[...] = c
        acc_ref[...] = acc

    @pl.when(j == 0)
    def _():
        key = lax.broadcasted_iota(jnp.int32, (ng * gr, page), 1)
        qry = lax.broadcasted_iota(jnp.int32, (ng * gr, page), 0) % tq
        blocks([(knew_ref, vnew_ref)], key < qry, jnp.zeros(c_ref.shape, F32),
               jnp.zeros(acc_ref.shape, F32))

    @pl.when(j > 0)
    def _():
        blocks([(kp_refs[p], vp_refs[p]) for p in reversed(range(pps))], None,
               c_ref[...], acc_ref[...])

    @pl.when(j == pl.num_programs(1) - 1)
    def _():
        o_ref[...] = acc_ref[...]


def sb_attention_sample(q, k_new, v_new, bias, cache_k, cache_v, page_table, li, *, heads):
    bd, tq, hd = q.shape
    dh = hd // heads
    page = cache_k.shape[2]
    n_pages = page_table.shape[1]
    assert heads % SUBLANES == 0 and tq <= page
    ng = heads // SUBLANES
    nr = heads * tq
    pps = next(p for p in (4, 2, 1) if n_pages % p == 0)
    n_groups = n_pages // pps
    q_rows = q.reshape(bd, tq, heads, dh).transpose(0, 2, 1, 3).reshape(bd, nr, dh).astype(BF16)
    bias_rows = jnp.broadcast_to(jnp.repeat(bias.astype(F32), tq)[:, None], (nr, page))
    e8 = np.kron(np.eye(page, dtype=np.float32), np.ones((1, SUBLANES), np.float32))
    e8, e8t = jnp.asarray(e8, BF16), jnp.asarray(e8.T, BF16)
    grouped = lambda c: c.reshape(*c.shape[:-2], ng, SUBLANES, dh)
    new = lambda x: grouped(jnp.pad(x.reshape(bd, tq, heads, dh),
                                    ((0, 0), (0, page - tq), (0, 0), (0, 0))))

    def page_spec(p):
        def index_map(b, j, pt):
            return (li, pt[b, (n_groups - jnp.maximum(j, 1)) * pps + p], 0, 0, 0, 0)
        return pl.BlockSpec((None, None, page, ng, SUBLANES, dh), index_map)

    const = lambda shape: pl.BlockSpec(shape, lambda b, j, pt: (0,) * len(shape))
    per_b = lambda shape: pl.BlockSpec((None,) + shape, lambda b, j, pt: (b,) + (0,) * len(shape))
    grid_spec = pltpu.PrefetchScalarGridSpec(
        num_scalar_prefetch=1, grid=(bd, n_groups + 1),
        in_specs=[const((nr, page)), per_b((nr, dh)),
                  const((page, page * SUBLANES)), const((page * SUBLANES, page)),
                  per_b((page, ng, SUBLANES, dh)), per_b((page, ng, SUBLANES, dh))]
                 + [page_spec(p) for p in range(pps)] * 2,
        out_specs=per_b((nr, dh)),
        scratch_shapes=[pltpu.VMEM((nr, 1), F32), pltpu.VMEM((nr, dh), F32)])
    o = pl.pallas_call(
        functools.partial(_sb_sample_kernel, tq=tq, page=page, pps=pps, scale=dh ** -0.5),
        grid_spec=grid_spec, out_shape=jax.ShapeDtypeStruct((bd, nr, dh), F32),
        compiler_params=_params("parallel", "arbitrary"), name="sb_sample",
    )(page_table, bias_rows, q_rows, e8, e8t, new(k_new), new(v_new),
      *([grouped(cache_k)] * pps), *([grouped(cache_v)] * pps))
    return o.reshape(bd, heads, tq, dh).transpose(0, 2, 1, 3).reshape(bd, tq, hd)


def _topk_rows(s, k):
    r = s.shape[0]
    rows = lax.broadcasted_iota(jnp.int32, s.shape, 0)
    vals, ids = [], []
    for _ in range(k):
        m = jnp.max(s, axis=0, keepdims=True)
        idx = jnp.min(jnp.where(s == m, rows, r), axis=0, keepdims=True)
        vals.append(m)
        ids.append(idx)
        s = jnp.where(rows == idx, -jnp.inf, s)
    return jnp.concatenate(vals, axis=0), jnp.concatenate(ids, axis=0)


def _peer_route_kernel(q_ref, sub_ref, e_ref, g_ref, *, nk, topk):
    dkh = sub_ref.shape[-1]
    q = q_ref[...]
    tops = []
    for p in range(2):
        qh, ql = _split_bf16(q[:, p * dkh:(p + 1) * dkh])
        sh, sl = _split_bf16(sub_ref[p])
        nt = lambda a, b: lax.dot_general(a, b, NT_DIMS, preferred_element_type=F32)
        s = nt(sh, qh) + nt(sh, ql) + nt(sl, qh)
        tops.append(_topk_rows(s, topk))
    (s1, i1), (s2, i2) = tops
    width = [topk // (i + 1) for i in range(topk)]
    cand = jnp.concatenate([s1[i:i + 1] + s2[:width[i]] for i in range(topk)], axis=0)
    cidx = jnp.concatenate([i1[i:i + 1] * nk + i2[:width[i]] for i in range(topk)], axis=0)
    rows = lax.broadcasted_iota(jnp.int32, cand.shape, 0)
    top, sel = _topk_rows(cand, topk)
    e_ref[...] = jnp.concatenate(
        [jnp.max(jnp.where(rows == sel[i:i + 1], cidx, -1), axis=0, keepdims=True)
         for i in range(topk)], axis=0)
    ex = jnp.exp(top - jnp.max(top, axis=0, keepdims=True))
    g_ref[...] = ex / jnp.sum(ex, axis=0, keepdims=True)


def peer_route(q, subkeys, layer):
    m, _ = q.shape
    _, heads, _, nk, dkh = subkeys.shape
    tt = _tile(m, 512)
    out = pl.BlockSpec((PEER_TOPK, tt), lambda i, h: (h, i))
    return pl.pallas_call(
        functools.partial(_peer_route_kernel, nk=nk, topk=PEER_TOPK),
        grid=(m // tt, heads),
        in_specs=[pl.BlockSpec((tt, 2 * dkh), lambda i, h: (i, h)),
                  pl.BlockSpec((None, None, 2, nk, dkh), lambda i, h: (layer, h, 0, 0, 0))],
        out_specs=[out, out],
        out_shape=[jax.ShapeDtypeStruct((heads * PEER_TOPK, m), jnp.int32),
                   jax.ShapeDtypeStruct((heads * PEER_TOPK, m), F32)],
        compiler_params=_params("parallel", "parallel"), name="peer_route")(q, subkeys)


def _peer_w_kernel(e_ref, g_ref, w_ref, *, nk):
    tb, hk = e_ref.shape
    key = lax.broadcasted_iota(jnp.int32, (nk, hk), 0)

    def body(t, _):
        e = e_ref[pl.ds(t, 1), :]
        g = g_ref[pl.ds(t, 1), :]
        a = jnp.where(key == e // nk, g, 0.0).astype(BF16)
        b = jnp.where(key == e % nk, 1.0, 0.0).astype(BF16)
        w = lax.dot_general(a, b, NT_DIMS, preferred_element_type=F32)
        w_ref[:, pl.ds(t, 1)] = w.reshape(nk // SUBLANES, 1, SUBLANES, nk)
        return 0

    lax.fori_loop(0, tb, body, 0, unroll=32)


def peer_gate_matrix(eidx, gate, nk):
    m, hk = eidx.shape
    tb = _tile(m, 128)
    row = pl.BlockSpec((tb, hk), lambda i: (i, 0))
    w = pl.pallas_call(
        functools.partial(_peer_w_kernel, nk=nk),
        grid=(m // tb,), in_specs=[row, row],
        out_specs=pl.BlockSpec((nk // SUBLANES, tb, SUBLANES, nk), lambda i: (0, i, 0, 0)),
        out_shape=jax.ShapeDtypeStruct((nk // SUBLANES, m, SUBLANES, nk), F32),
        compiler_params=_params("parallel"), name="peer_w")(eidx, gate)
    return w.reshape(nk // SUBLANES, m * SUBLANES, nk)


def _peer_scores_kernel(h_ref, u_ref, w_ref, p_ref):
    tm = h_ref.shape[0]
    nk = w_ref.shape[1]
    act = _gelu(lax.dot_general(h_ref[...], u_ref[...], NT_DIMS, preferred_element_type=F32))
    for a in range(SUBLANES):
        cols = slice(a * nk, (a + 1) * nk)
        p_ref[:, cols] = (w_ref[pl.ds(a, tm, stride=SUBLANES), :] * act[:, cols]).astype(p_ref.dtype)


def peer_scores(h, u, w, layer):
    m, d = h.shape
    e = u.shape[1]
    nk = w.shape[2]
    tm = _tile(m, 1024)
    te = SUBLANES * nk
    assert e == nk * nk and nk % SUBLANES == 0
    return pl.pallas_call(
        _peer_scores_kernel, grid=(m // tm, e // te),
        in_specs=[pl.BlockSpec((tm, d), lambda i, j: (i, 0)),
                  pl.BlockSpec((None, te, d), lambda i, j: (layer, j, 0)),
                  pl.BlockSpec((None, tm * SUBLANES, nk), lambda i, j: (j, i, 0))],
        out_specs=pl.BlockSpec((tm, te), lambda i, j: (i, j)),
        out_shape=jax.ShapeDtypeStruct((m, e), BF16),
        compiler_params=_params("parallel", "parallel"), name="peer_scores")(h, u, w)


def _peer_combine_kernel(x_ref, p_ref, v_ref, o_ref):
    @pl.when(pl.program_id(2) == 0)
    def _():
        o_ref[...] = x_ref[...]

    o_ref[...] += jnp.dot(p_ref[...], v_ref[...], preferred_element_type=F32)


def peer_combine(x, p, v, layer):
    m, d = x.shape
    e = p.shape[1]
    tm, tn, tk = _tile(m, 1024), _tile(d, 1024), _tile(e, 4096)
    tile = pl.BlockSpec((tm, tn), lambda i, j, k: (i, j))
    return pl.pallas_call(
        _peer_combine_kernel, grid=(m // tm, d // tn, e // tk),
        in_specs=[tile, pl.BlockSpec((tm, tk), lambda i, j, k: (i, k)),
                  pl.BlockSpec((None, tk, tn), lambda i, j, k: (layer, k, j))],
        out_specs=tile, out_shape=jax.ShapeDtypeStruct((m, d), F32),
        compiler_params=_params("parallel", "parallel", "arbitrary"), name="peer_combine")(x, p, v)


def peer(x, g, w_q, subkeys, u, v, layer):
    m = x.shape[0]
    nk = subkeys.shape[3]
    h = rmsnorm(x, g)
    q = matmul(h, w_q, layer)
    mp = -(-m // LANES) * LANES
    e_t, g_t = peer_route(jnp.pad(q, ((0, mp - m), (0, 0))), subkeys, layer)
    w = peer_gate_matrix(e_t.T[:m], g_t.T[:m], nk)
    return peer_combine(x, peer_scores(h, u, w, layer), v, layer)


def kernel(x_prompt, x_sample, cache_k, cache_v, page_table, norm_mix, norm_ffn, norm_final,
           cm_w_in, cm_norm, cm_w_s, cm_b_s, cm_w_out, sb_w_qkv, sb_bias, sb_w_o,
           peer_w_q, peer_subkeys, peer_u, peer_v):
    b, t, d = x_prompt.shape
    bd, ts, _ = x_sample.shape
    depth = norm_mix.shape[0]
    heads = cache_k.shape[3]
    hd = heads * cache_k.shape[4]
    xp = x_prompt.reshape(b * t, d)
    xs = x_sample.reshape(bd * ts, d)
    assert t % CHUNK == 0 and ts <= CHUNK
    to_bf = lambda a: a.astype(BF16)
    cm_w_in, cm_w_out, sb_w_qkv, sb_w_o, peer_w_q, peer_u, peer_v = map(
        to_bf, (cm_w_in, cm_w_out, sb_w_qkv, sb_w_o, peer_w_q, peer_u, peer_v))

    gate_p, gate_s, kp_rows, vp_rows, ks_rows, vs_rows = [], [], [], [], [], []
    for layer in range(depth):
        li = layer // 2
        hp, hs = rmsnorm(xp, norm_mix[layer]), rmsnorm(xs, norm_mix[layer])
        if layer % 2 == 0:
            dg = cm_w_out.shape[1]
            zp = matmul(hp, cm_w_in, li, gelu=True)
            zs = matmul(hs, cm_w_in, li, gelu=True)
            pp, vnp = gating(zp, cm_norm[li], cm_w_s[li], cm_b_s[li])
            zs = jnp.pad(zs.reshape(bd, ts, 2 * dg), ((0, 0), (0, CHUNK - ts), (0, 0)))
            ps, vns = gating(zs.reshape(bd * CHUNK, 2 * dg), cm_norm[li], cm_w_s[li], cm_b_s[li])
            ps = ps.reshape(bd, CHUNK, dg)[:, :ts].reshape(bd * ts, dg)
            gate_p.append(vnp.reshape(b, t, dg)[:, t - CHUNK:])
            gate_s.append(vns.reshape(bd, CHUNK, dg)[:, :ts])
            xp = matmul(pp, cm_w_out, li, res=xp)
            xs = matmul(ps, cm_w_out, li, res=xs)
        else:
            qp = matmul(hp, sb_w_qkv, li, n=hd, col0=0, out_dtypes=(BF16,))
            kp, kpb = matmul(hp, sb_w_qkv, li, n=hd, col0=hd, out_dtypes=(F32, BF16))
            vp, vpb = matmul(hp, sb_w_qkv, li, n=hd, col0=2 * hd, out_dtypes=(F32, BF16))
            op = sb_attention_prompt(qp, kpb, vpb, sb_bias[li], batch=b, heads=heads)
            xp = matmul(op, sb_w_o, li, res=xp)
            qkv_s = matmul(hs, sb_w_qkv, li).reshape(bd, ts, 3, hd)
            qs, kq, vq = qkv_s[:, :, 0], qkv_s[:, :, 1], qkv_s[:, :, 2]
            os_ = sb_attention_sample(qs, kq, vq, sb_bias[li], cache_k, cache_v, page_table,
                                      li, heads=heads)
            xs = matmul(os_.reshape(bd * ts, hd).astype(BF16), sb_w_o, li, res=xs)
            shape5 = lambda a, n0, n1: a.reshape(n0, n1, heads, hd // heads)
            kp_rows.append(shape5(kp, b, t))
            vp_rows.append(shape5(vp, b, t))
            ks_rows.append(shape5(kq, bd, ts))
            vs_rows.append(shape5(vq, bd, ts))
        xp = peer(xp, norm_ffn[layer], peer_w_q, peer_subkeys, peer_u, peer_v, layer)
        xs = peer(xs, norm_ffn[layer], peer_w_q, peer_subkeys, peer_u, peer_v, layer)
    y_prompt = rmsnorm(xp, norm_final, out_dtype=F32).reshape(b, t, d)
    y_sample = rmsnorm(xs, norm_final, out_dtype=F32).reshape(bd, ts, d)
    return (y_prompt, y_sample, jnp.stack(gate_p), jnp.stack(gate_s),
            jnp.stack(kp_rows), jnp.stack(vp_rows), jnp.stack(ks_rows), jnp.stack(vs_rows))
```

```python
import functools

import jax
import jax.numpy as jnp
import numpy as np
from jax import lax
from jax.experimental import pallas as pl
from jax.experimental.pallas import tpu as pltpu

EPS = 1e-6
CHUNK = 128
PEER_TOPK = 16
V7X_VMEM_LIMIT_BYTES = 56 * 1024 * 1024
LANES = 128
SUBLANES = 8
F32 = jnp.float32
BF16 = jnp.bfloat16
SQRT_HALF = float(np.sqrt(0.5))
NT_DIMS = (((1,), (1,)), ((), ()))


def _params(*sem):
    return pltpu.CompilerParams(dimension_semantics=sem,
                                vmem_limit_bytes=V7X_VMEM_LIMIT_BYTES)


def _gelu(x):
    return 0.5 * x * (1.0 + lax.erf(x * SQRT_HALF))


def _split_bf16(x):
    hi = x.astype(BF16)
    lo = (x - hi.astype(F32)).astype(BF16)
    return hi, lo


def _tile(n, pref):
    return pref if n % pref == 0 else n


def _rmsnorm_kernel(x_ref, g_ref, o_ref):
    x = x_ref[...]
    ms = jnp.mean(x * x, axis=-1, keepdims=True)
    o_ref[...] = (x * lax.rsqrt(ms + EPS) * g_ref[...]).astype(o_ref.dtype)


def rmsnorm(x, g, *, out_dtype=BF16):
    m, d = x.shape
    tr = _tile(m, 256)
    row = pl.BlockSpec((tr, d), lambda i: (i, 0))
    return pl.pallas_call(
        _rmsnorm_kernel, grid=(m // tr,),
        in_specs=[row, pl.BlockSpec((1, d), lambda i: (0, 0))], out_specs=row,
        out_shape=jax.ShapeDtypeStruct((m, d), out_dtype),
        compiler_params=_params("parallel"), name="rmsnorm")(x, g.reshape(1, d))


def _matmul_kernel(*refs, has_res, gelu):
    a_ref, w_ref = refs[:2]
    outs = refs[3:] if has_res else refs[2:]
    acc = jnp.dot(a_ref[...], w_ref[...], preferred_element_type=F32)
    if has_res:
        acc = refs[2][...] + acc
    if gelu:
        acc = _gelu(acc)
    for o in outs:
        o[...] = acc.astype(o.dtype)


def matmul(a, w, layer, *, n=None, col0=0, res=None, gelu=False, out_dtypes=(F32,)):
    m, k = a.shape
    n = w.shape[2] if n is None else n
    tm = _tile(m, 1024)
    tn = _tile(n, 512 if (res is not None or len(out_dtypes) > 1) else 1024)
    assert col0 % tn == 0
    cb = col0 // tn
    ins = [a, w] + ([res] if res is not None else [])
    tile = pl.BlockSpec((tm, tn), lambda i, j: (i, j))
    in_specs = [pl.BlockSpec((tm, k), lambda i, j: (i, 0)),
                pl.BlockSpec((None, k, tn), lambda i, j: (layer, 0, j + cb))]
    if res is not None:
        in_specs.append(tile)
    outs = pl.pallas_call(
        functools.partial(_matmul_kernel, has_res=res is not None, gelu=gelu),
        grid=(m // tm, n // tn), in_specs=in_specs,
        out_specs=[tile] * len(out_dtypes),
        out_shape=[jax.ShapeDtypeStruct((m, n), dt) for dt in out_dtypes],
        compiler_params=_params("parallel", "parallel"), name="matmul")(*ins)
    return outs if len(out_dtypes) > 1 else outs[0]


def _gating_kernel(u_ref, v_ref, gn_ref, ws_ref, b_ref, p_ref, vn_ref, *, groups, gd):
    v = v_ref[...]
    ms = jnp.mean(v * v, axis=-1, keepdims=True)
    vn = v * lax.rsqrt(ms + EPS) * gn_ref[...]
    vn_ref[...] = vn
    t_i = lax.broadcasted_iota(jnp.int32, (CHUNK, CHUNK), 0)
    s_i = lax.broadcasted_iota(jnp.int32, (CHUNK, CHUNK), 1)
    causal = s_i <= t_i
    for g in range(groups):
        cols = slice(g * gd, (g + 1) * gd)
        wm = jnp.where(causal, ws_ref[g], 0.0).astype(BF16)
        s = jnp.dot(wm, vn[:, cols].astype(BF16), preferred_element_type=F32)
        s = s + b_ref[:, cols]
        p_ref[:, cols] = (u_ref[:, cols] * s).astype(p_ref.dtype)


def gating(z, g_norm, w_s, b_s):
    rows, dg2 = z.shape
    dg = dg2 // 2
    groups = w_s.shape[0]
    gd = dg // groups
    b_full = jnp.repeat(b_s.T, gd, axis=1)
    half = lambda c: pl.BlockSpec((CHUNK, dg), lambda i: (i, c))
    return pl.pallas_call(
        functools.partial(_gating_kernel, groups=groups, gd=gd),
        grid=(rows // CHUNK,),
        in_specs=[half(0), half(1),
                  pl.BlockSpec((1, dg), lambda i: (0, 0)),
                  pl.BlockSpec((groups, CHUNK, CHUNK), lambda i: (0, 0, 0)),
                  pl.BlockSpec((CHUNK, dg), lambda i: (0, 0))],
        out_specs=[half(0), half(0)],
        out_shape=[jax.ShapeDtypeStruct((rows, dg), BF16),
                   jax.ShapeDtypeStruct((rows, dg), F32)],
        compiler_params=_params("parallel"), name="gating",
    )(z, z, g_norm.reshape(1, dg), w_s, b_full)


def _sb_log_weights(z, tri, mask):
    lg = jnp.log(1.0 + jnp.exp(-jnp.abs(z)))
    l1m = jnp.minimum(-z, 0.0) - lg
    ls = z + l1m
    if mask is not None:
        l1m = jnp.where(mask, l1m, 0.0)
    hi, lo = _split_bf16(l1m)
    excl = (jnp.dot(hi, tri, preferred_element_type=F32)
            + jnp.dot(lo, tri, preferred_element_type=F32))
    pre = ls + excl
    if mask is not None:
        pre = jnp.where(mask, pre, -jnp.inf)
    return pre, jnp.sum(l1m, axis=1, keepdims=True)


def _sb_prompt_kernel(bias_ref, q_ref, k_ref, v_ref, o_ref, *, t, dh, hps, scale):
    h0 = pl.program_id(1) * hps
    nq = q_ref.shape[0] // t
    r_i = lax.broadcasted_iota(jnp.int32, (t, t), 0)
    c_i = lax.broadcasted_iota(jnp.int32, (t, t), 1)
    tri = jnp.where(r_i > c_i, 1.0, 0.0).astype(BF16)
    cols = [slice(x * dh, (x + 1) * dh) for x in range(hps)]
    bias = [bias_ref[h0 + x] for x in range(hps)]

    def log_weights(qi, kb, mask):
        ks = pl.multiple_of(kb * t, t)
        out = []
        for x in range(hps):
            q = (q_ref[qi * t:(qi + 1) * t, cols[x]].astype(F32) * scale).astype(BF16)
            z = lax.dot_general(q, k_ref[pl.ds(ks, t), cols[x]], NT_DIMS,
                                preferred_element_type=F32) + bias[x]
            out.append(_sb_log_weights(z, tri, mask))
        return tuple(out)

    def accumulate(kb, logw, carry):
        ks = pl.multiple_of(kb * t, t)
        out = []
        for x, ((pre, rowsum), (c, acc)) in enumerate(zip(logw, carry)):
            a = jnp.exp(pre + c).astype(BF16)
            acc = acc + jnp.dot(a, v_ref[pl.ds(ks, t), cols[x]], preferred_element_type=F32)
            out.append((c + rowsum, acc))
        return tuple(out)

    zero = (jnp.zeros((t, 1), F32), jnp.zeros((t, dh), F32))
    logw = log_weights(0, 0, c_i < r_i)
    for qi in range(nq):
        def body(i, state, qi=qi):
            logw, carry = state
            return log_weights(qi, qi - 1 - i, None), accumulate(qi - i, logw, carry)

        logw, carry = lax.fori_loop(0, qi, body, (logw, (zero,) * hps))
        nxt = log_weights(qi + 1, qi + 1, c_i < r_i) if qi + 1 < nq else None
        for x, (_, acc) in enumerate(accumulate(0, logw, carry)):
            o_ref[qi * t:(qi + 1) * t, cols[x]] = acc.astype(o_ref.dtype)
        logw = nxt


def sb_attention_prompt(q, k, v, bias, *, batch, heads):
    m, hd = q.shape
    t = m // batch
    dh = hd // heads
    hps = next(n for n in (4, 2, 1) if heads % n == 0)
    seq = pl.BlockSpec((t, hps * dh), lambda b, h: (b, h))
    return pl.pallas_call(
        functools.partial(_sb_prompt_kernel, t=_tile(t, 256), dh=dh, hps=hps, scale=dh ** -0.5),
        grid=(batch, heads // hps),
        in_specs=[pl.BlockSpec(memory_space=pltpu.SMEM), seq, seq, seq],
        out_specs=seq,
        out_shape=jax.ShapeDtypeStruct((m, hd), BF16),
        compiler_params=_params("parallel", "parallel"), name="sb_prompt",
    )(bias.astype(F32), q, k, v)


def _sb_sample_kernel(pt_ref, bias_ref, q_ref, e8_ref, e8t_ref, knew_ref, vnew_ref, *rest,
                      tq, page, pps, scale):
    del pt_ref
    kp_refs, vp_refs = rest[:pps], rest[pps:2 * pps]
    o_ref, c_ref, acc_ref = rest[2 * pps:]
    j = pl.program_id(1)
    ng, dh = knew_ref.shape[1], knew_ref.shape[3]
    gr = SUBLANES * tq
    lanes = page * SUBLANES
    j_i = lax.broadcasted_iota(jnp.int32, (page, page), 0)
    s_i = lax.broadcasted_iota(jnp.int32, (page, page), 1)
    tri = jnp.where(j_i > s_i, 1.0, 0.0).astype(BF16)
    own = (lax.broadcasted_iota(jnp.int32, (gr, lanes), 1) % SUBLANES
           == lax.broadcasted_iota(jnp.int32, (gr, lanes), 0) // tq)

    def blocks(kv_refs, mask, c, acc):
        nb, nr = len(kv_refs), ng * gr
        ss = []
        for k_ref, _ in kv_refs:
            for g in range(ng):
                kg = k_ref[:, g].reshape(lanes, dh).astype(BF16)
                s = lax.dot_general(q_ref[g * gr:(g + 1) * gr, :], kg, NT_DIMS,
                                    preferred_element_type=F32)
                ss.append(jnp.where(own, s, 0.0))
        hi_lo = jnp.concatenate(_split_bf16(jnp.concatenate(ss, axis=0)), axis=0)
        z2 = jnp.dot(hi_lo, e8t_ref[...], preferred_element_type=F32)
        z = ((z2[:nb * nr] + z2[nb * nr:]) * scale
             + jnp.concatenate([bias_ref[...]] * nb, axis=0))
        pre, rowsum = _sb_log_weights(z, tri, mask)
        a = []
        for b in range(nb):
            rows = slice(b * nr, (b + 1) * nr)
            a.append(jnp.exp(pre[rows] + c).astype(BF16))
            c = c + rowsum[rows]
        spread = jnp.dot(jnp.concatenate(a, axis=0), e8_ref[...], preferred_element_type=F32)
        for b, (_, v_ref) in enumerate(kv_refs):
            outs = []
            for g in range(ng):
                r0 = b * nr + g * gr
                ae = jnp.where(own, spread[r0:r0 + gr], 0.0).astype(BF16)
                vg = v_ref[:, g].reshape(lanes, dh).astype(BF16)
                outs.append(jnp.dot(ae, vg, preferred_element_type=F32))
            acc = acc + jnp.concatenate(outs, axis=0)
        c_ref[...] = c
        acc_ref[...] = acc

    @pl.when(j == 0)
    def _():
        key = lax.broadcasted_iota(jnp.int32, (ng * gr, page), 1)
        qry = lax.broadcasted_iota(jnp.int32, (ng * gr, page), 0) % tq
        blocks([(knew_ref, vnew_ref)], key < qry, jnp.zeros(c_ref.shape, F32),
               jnp.zeros(acc_ref.shape, F32))

    @pl.when(j > 0)
    def _():
        blocks([(kp_refs[p], vp_refs[p]) for p in reversed(range(pps))], None,
               c_ref[...], acc_ref[...])

    @pl.when(j == pl.num_programs(1) - 1)
    def _():
        o_ref[...] = acc_ref[...]


def sb_attention_sample(q, k_new, v_new, bias, cache_k, cache_v, page_table, li, *, heads):
    bd, tq, hd = q.shape
    dh = hd // heads
    page = cache_k.shape[2]
    n_pages = page_table.shape[1]
    assert heads % SUBLANES == 0 and tq <= page
    ng = heads // SUBLANES
    nr = heads * tq
    pps = next(p for p in (4, 2, 1) if n_pages % p == 0)
    n_groups = n_pages // pps
    q_rows = q.reshape(bd, tq, heads, dh).transpose(0, 2, 1, 3).reshape(bd, nr, dh).astype(BF16)
    bias_rows = jnp.broadcast_to(jnp.repeat(bias.astype(F32), tq)[:, None], (nr, page))
    e8 = np.kron(np.eye(page, dtype=np.float32), np.ones((1, SUBLANES), np.float32))
    e8, e8t = jnp.asarray(e8, BF16), jnp.asarray(e8.T, BF16)
    grouped = lambda c: c.reshape(*c.shape[:-2], ng, SUBLANES, dh)
    new = lambda x: grouped(jnp.pad(x.reshape(bd, tq, heads, dh),
                                    ((0, 0), (0, page - tq), (0, 0), (0, 0))))

    def page_spec(p):
        def index_map(b, j, pt):
            return (li, pt[b, (n_groups - jnp.maximum(j, 1)) * pps + p], 0, 0, 0, 0)
        return pl.BlockSpec((None, None, page, ng, SUBLANES, dh), index_map)

    const = lambda shape: pl.BlockSpec(shape, lambda b, j, pt: (0,) * len(shape))
    per_b = lambda shape: pl.BlockSpec((None,) + shape, lambda b, j, pt: (b,) + (0,) * len(shape))
    grid_spec = pltpu.PrefetchScalarGridSpec(
        num_scalar_prefetch=1, grid=(bd, n_groups + 1),
        in_specs=[const((nr, page)), per_b((nr, dh)),
                  const((page, page * SUBLANES)), const((page * SUBLANES, page)),
                  per_b((page, ng, SUBLANES, dh)), per_b((page, ng, SUBLANES, dh))]
                 + [page_spec(p) for p in range(pps)] * 2,
        out_specs=per_b((nr, dh)),
        scratch_shapes=[pltpu.VMEM((nr, 1), F32), pltpu.VMEM((nr, dh), F32)])
    o = pl.pallas_call(
        functools.partial(_sb_sample_kernel, tq=tq, page=page, pps=pps, scale=dh ** -0.5),
        grid_spec=grid_spec, out_shape=jax.ShapeDtypeStruct((bd, nr, dh), F32),
        compiler_params=_params("parallel", "arbitrary"), name="sb_sample",
    )(page_table, bias_rows, q_rows, e8, e8t, new(k_new), new(v_new),
      *([grouped(cache_k)] * pps), *([grouped(cache_v)] * pps))
    return o.reshape(bd, heads, tq, dh).transpose(0, 2, 1, 3).reshape(bd, tq, hd)


def _topk_rows(s, k):
    r = s.shape[0]
    rows = lax.broadcasted_iota(jnp.int32, s.shape, 0)
    vals, ids = [], []
    for _ in range(k):
        m = jnp.max(s, axis=0, keepdims=True)
        idx = jnp.min(jnp.where(s == m, rows, r), axis=0, keepdims=True)
        vals.append(m)
        ids.append(idx)
        s = jnp.where(rows == idx, -jnp.inf, s)
    return jnp.concatenate(vals, axis=0), jnp.concatenate(ids, axis=0)


def _peer_route_kernel(q_ref, sub_ref, e_ref, g_ref, *, nk, topk):
    dkh = sub_ref.shape[-1]
    q = q_ref[...]
    tops = []
    for p in range(2):
        qh, ql = _split_bf16(q[:, p * dkh:(p + 1) * dkh])
        sh, sl = _split_bf16(sub_ref[p])
        nt = lambda a, b: lax.dot_general(a, b, NT_DIMS, preferred_element_type=F32)
        s = nt(sh, qh) + nt(sh, ql) + nt(sl, qh)
        tops.append(_topk_rows(s, topk))
    (s1, i1), (s2, i2) = tops
    width = [topk // (i + 1) for i in range(topk)]
    cand = jnp.concatenate([s1[i:i + 1] + s2[:width[i]] for i in range(topk)], axis=0)
    cidx = jnp.concatenate([i1[i:i + 1] * nk + i2[:width[i]] for i in range(topk)], axis=0)
    rows = lax.broadcasted_iota(jnp.int32, cand.shape, 0)
    top, sel = _topk_rows(cand, topk)
    e_ref[...] = jnp.concatenate(
        [jnp.max(jnp.where(rows == sel[i:i + 1], cidx, -1), axis=0, keepdims=True)
         for i in range(topk)], axis=0)
    ex = jnp.exp(top - jnp.max(top, axis=0, keepdims=True))
    g_ref[...] = ex / jnp.sum(ex, axis=0, keepdims=True)


def peer_route(q, subkeys, layer):
    m, _ = q.shape
    _, heads, _, nk, dkh = subkeys.shape
    tt = _tile(m, 512)
    out = pl.BlockSpec((PEER_TOPK, tt), lambda i, h: (h, i))
    return pl.pallas_call(
        functools.partial(_peer_route_kernel, nk=nk, topk=PEER_TOPK),
        grid=(m // tt, heads),
        in_specs=[pl.BlockSpec((tt, 2 * dkh), lambda i, h: (i, h)),
                  pl.BlockSpec((None, None, 2, nk, dkh), lambda i, h: (layer, h, 0, 0, 0))],
        out_specs=[out, out],
        out_shape=[jax.ShapeDtypeStruct((heads * PEER_TOPK, m), jnp.int32),
                   jax.ShapeDtypeStruct((heads * PEER_TOPK, m), F32)],
        compiler_params=_params("parallel", "parallel"), name="peer_route")(q, subkeys)


def _peer_w_kernel(e_ref, g_ref, w_ref, *, nk):
    tb, hk = e_ref.shape
    key = lax.broadcasted_iota(jnp.int32, (nk, hk), 0)

    def body(t, _):
        e = e_ref[pl.ds(t, 1), :]
        g = g_ref[pl.ds(t, 1), :]
        a = jnp.where(key == e // nk, g, 0.0).astype(BF16)
        b = jnp.where(key == e % nk, 1.0, 0.0).astype(BF16)
        w = lax.dot_general(a, b, NT_DIMS, preferred_element_type=F32)
        w_ref[:, pl.ds(t, 1)] = w.reshape(nk // SUBLANES, 1, SUBLANES, nk)
        return 0

    lax.fori_loop(0, tb, body, 0, unroll=32)


def peer_gate_matrix(eidx, gate, nk):
    m, hk = eidx.shape
    tb = _tile(m, 128)
    row = pl.BlockSpec((tb, hk), lambda i: (i, 0))
    w = pl.pallas_call(
        functools.partial(_peer_w_kernel, nk=nk),
        grid=(m // tb,), in_specs=[row, row],
        out_specs=pl.BlockSpec((nk // SUBLANES, tb, SUBLANES, nk), lambda i: (0, i, 0, 0)),
        out_shape=jax.ShapeDtypeStruct((nk // SUBLANES, m, SUBLANES, nk), F32),
        compiler_params=_params("parallel"), name="peer_w")(eidx, gate)
    return w.reshape(nk // SUBLANES, m * SUBLANES, nk)


def _peer_scores_kernel(h_ref, u_ref, w_ref, p_ref):
    tm = h_ref.shape[0]
    nk = w_ref.shape[1]
    act = _gelu(lax.dot_general(h_ref[...], u_ref[...], NT_DIMS, preferred_element_type=F32))
    for a in range(SUBLANES):
        cols = slice(a * nk, (a + 1) * nk)
        p_ref[:, cols] = (w_ref[pl.ds(a, tm, stride=SUBLANES), :] * act[:, cols]).astype(p_ref.dtype)


def peer_scores(h, u, w, layer):
    m, d = h.shape
    e = u.shape[1]
    nk = w.shape[2]
    tm = _tile(m, 1024)
    te = SUBLANES * nk
    assert e == nk * nk and nk % SUBLANES == 0
    return pl.pallas_call(
        _peer_scores_kernel, grid=(m // tm, e // te),
        in_specs=[pl.BlockSpec((tm, d), lambda i, j: (i, 0)),
                  pl.BlockSpec((None, te, d), lambda i, j: (layer, j, 0)),
                  pl.BlockSpec((None, tm * SUBLANES, nk), lambda i, j: (j, i, 0))],
        out_specs=pl.BlockSpec((tm, te), lambda i, j: (i, j)),
        out_shape=jax.ShapeDtypeStruct((m, e), BF16),
        compiler_params=_params("parallel", "parallel"), name="peer_scores")(h, u, w)


def _peer_combine_kernel(x_ref, p_ref, v_ref, o_ref):
    @pl.when(pl.program_id(2) == 0)
    def _():
        o_ref[...] = x_ref[...]

    o_ref[...] += jnp.dot(p_ref[...], v_ref[...], preferred_element_type=F32)


def peer_combine(x, p, v, layer):
    m, d = x.shape
    e = p.shape[1]
    tm, tn, tk = _tile(m, 1024), _tile(d, 1024), _tile(e, 4096)
    tile = pl.BlockSpec((tm, tn), lambda i, j, k: (i, j))
    return pl.pallas_call(
        _peer_combine_kernel, grid=(m // tm, d // tn, e // tk),
        in_specs=[tile, pl.BlockSpec((tm, tk), lambda i, j, k: (i, k)),
                  pl.BlockSpec((None, tk, tn), lambda i, j, k: (layer, k, j))],
        out_specs=tile, out_shape=jax.ShapeDtypeStruct((m, d), F32),
        compiler_params=_params("parallel", "parallel", "arbitrary"), name="peer_combine")(x, p, v)


def peer(x, g, w_q, subkeys, u, v, layer):
    m = x.shape[0]
    nk = subkeys.shape[3]
    h = rmsnorm(x, g)
    q = matmul(h, w_q, layer)
    mp = -(-m // LANES) * LANES
    e_t, g_t = peer_route(jnp.pad(q, ((0, mp - m), (0, 0))), subkeys, layer)
    w = peer_gate_matrix(e_t.T[:m], g_t.T[:m], nk)
    return peer_combine(x, peer_scores(h, u, w, layer), v, layer)


def kernel(x_prompt, x_sample, cache_k, cache_v, page_table, norm_mix, norm_ffn, norm_final,
           cm_w_in, cm_norm, cm_w_s, cm_b_s, cm_w_out, sb_w_qkv, sb_bias, sb_w_o,
           peer_w_q, peer_subkeys, peer_u, peer_v):
    b, t, d = x_prompt.shape
    bd, ts, _ = x_sample.shape
    depth = norm_mix.shape[0]
    heads = cache_k.shape[3]
    hd = heads * cache_k.shape[4]
    xp = x_prompt.reshape(b * t, d)
    xs = x_sample.reshape(bd * ts, d)
    assert t % CHUNK == 0 and ts <= CHUNK
    to_bf = lambda a: a.astype(BF16)
    cm_w_in, cm_w_out, sb_w_qkv, sb_w_o, peer_w_q, peer_u, peer_v = map(
        to_bf, (cm_w_in, cm_w_out, sb_w_qkv, sb_w_o, peer_w_q, peer_u, peer_v))

    gate_p, gate_s, kp_rows, vp_rows, ks_rows, vs_rows = [], [], [], [], [], []
    for layer in range(depth):
        li = layer // 2
        hp, hs = rmsnorm(xp, norm_mix[layer]), rmsnorm(xs, norm_mix[layer])
        if layer % 2 == 0:
            dg = cm_w_out.shape[1]
            zp = matmul(hp, cm_w_in, li, gelu=True)
            zs = matmul(hs, cm_w_in, li, gelu=True)
            pp, vnp = gating(zp, cm_norm[li], cm_w_s[li], cm_b_s[li])
            zs = jnp.pad(zs.reshape(bd, ts, 2 * dg), ((0, 0), (0, CHUNK - ts), (0, 0)))
            ps, vns = gating(zs.reshape(bd * CHUNK, 2 * dg), cm_norm[li], cm_w_s[li], cm_b_s[li])
            ps = ps.reshape(bd, CHUNK, dg)[:, :ts].reshape(bd * ts, dg)
            gate_p.append(vnp.reshape(b, t, dg)[:, t - CHUNK:])
            gate_s.append(vns.reshape(bd, CHUNK, dg)[:, :ts])
            xp = matmul(pp, cm_w_out, li, res=xp)
            xs = matmul(ps, cm_w_out, li, res=xs)
        else:
            qp = matmul(hp, sb_w_qkv, li, n=hd, col0=0, out_dtypes=(BF16,))
            kp, kpb = matmul(hp, sb_w_qkv, li, n=hd, col0=hd, out_dtypes=(F32, BF16))
            vp, vpb = matmul(hp, sb_w_qkv, li, n=hd, col0=2 * hd, out_dtypes=(F32, BF16))
            op = sb_attention_prompt(qp, kpb, vpb, sb_bias[li], batch=b, heads=heads)
            xp = matmul(op, sb_w_o, li, res=xp)
            qkv_s = matmul(hs, sb_w_qkv, li).reshape(bd, ts, 3, hd)
            qs, kq, vq = qkv_s[:, :, 0], qkv_s[:, :, 1], qkv_s[:, :, 2]
            os_ = sb_attention_sample(qs, kq, vq, sb_bias[li], cache_k, cache_v, page_table,
                                      li, heads=heads)
            xs = matmul(os_.reshape(bd * ts, hd).astype(BF16), sb_w_o, li, res=xs)
            shape5 = lambda a, n0, n1: a.reshape(n0, n1, heads, hd // heads)
            kp_rows.append(shape5(kp, b, t))
            vp_rows.append(shape5(vp, b, t))
            ks_rows.append(shape5(kq, bd, ts))
            vs_rows.append(shape5(vq, bd, ts))
        xp = peer(xp, norm_ffn[layer], peer_w_q, peer_subkeys, peer_u, peer_v, layer)
        xs = peer(xs, norm_ffn[layer], peer_w_q, peer_subkeys, peer_u, peer_v, layer)
    y_prompt = rmsnorm(xp, norm_final, out_dtype=F32).reshape(b, t, d)
    y_sample = rmsnorm(xs, norm_final, out_dtype=F32).reshape(bd, ts, d)
    return (y_prompt, y_sample, jnp.stack(gate_p), jnp.stack(gate_s),
            jnp.stack(kp_rows), jnp.stack(vp_rows), jnp.stack(ks_rows), jnp.stack(vs_rows))
```
